```python
import jax, jax.numpy as jnp
from jax import lax
import numpy as np

D_MODEL = 2048
BATCH = 2
SEQ = 16384
DEPTH = 2

CHUNK = 64
CONV_W = 4
A_HEADS = 8
A_DK = 128
A_DV = 128
A_QK = A_HEADS * A_DK
A_VW = A_HEADS * A_DV
B_HEADS = 4
B_DK = 128
B_DV = 256
B_QK = B_HEADS * B_DK
B_VW = B_HEADS * B_DV
B_GATE_RANK = 16
B_GATE_NORM = 16.0
C_HEADS = 16
C_HEADDIM = 64
C_GROUPS = 2
C_DSTATE = 128
C_DINNER = C_HEADS * C_HEADDIM
C_XBC = C_DINNER + 2 * C_GROUPS * C_DSTATE
N_BRANCH = 3
N_GROUPS = 4
EXPERTS_PER_GROUP = 4
N_EXPERTS = N_GROUPS * EXPERTS_PER_GROUP
TOP_K = 2
D_EXPERT = 768
LN_EPS = 1e-5
RMS_EPS = 1e-6
ALPHA = (2 * DEPTH) ** 0.25
BETA = (8 * DEPTH) ** -0.25
IN_SPLITS = (A_QK, A_QK, A_VW, A_VW, A_HEADS, A_HEADS,
             B_QK, B_QK, B_VW, B_VW, B_GATE_RANK,
             C_DINNER, C_XBC, C_HEADS,
             N_BRANCH * D_MODEL)
N_IN = sum(IN_SPLITS)

kernel_name = 'hybrid_gdn_gla_ssd_hiermoe_deepnorm'


def rmsnorm(x, g):
    xf = x.astype(jnp.float32)
    return xf * lax.rsqrt(jnp.mean(xf * xf, axis=-1, keepdims=True) + RMS_EPS) * g.astype(jnp.float32)


def layernorm(x, g, b):
    xf = x.astype(jnp.float32)
    mu = jnp.mean(xf, axis=-1, keepdims=True)
    xc = xf - mu
    var = jnp.mean(xc * xc, axis=-1, keepdims=True)
    return (xc * lax.rsqrt(var + LN_EPS) * g.astype(jnp.float32) + b.astype(jnp.float32)).astype(x.dtype)


def l2norm(x):
    return x * lax.rsqrt(jnp.sum(x * x, axis=-1, keepdims=True) + 1e-6)


def causal_dwconv(x, w, b=None):
    c = x.shape[-1]
    y = lax.conv_general_dilated(x, w[:, None, :].astype(x.dtype), window_strides=(1,),
                                 padding=[(w.shape[0] - 1, 0)],
                                 dimension_numbers=('NWC', 'WIO', 'NWC'),
                                 feature_group_count=c)
    if b is not None:
        y = y + b.astype(x.dtype)
    return y


def to_chunks(t):
    bsz, length, h = t.shape[:3]
    t = t.reshape((bsz, length // CHUNK, CHUNK, h) + t.shape[3:])
    return jnp.moveaxis(t, 3, 1)


def from_chunks(t):
    t = jnp.moveaxis(t, 1, 3)
    return t.reshape((t.shape[0], t.shape[1] * t.shape[2]) + t.shape[3:])


def gated_delta_chunked(q, k, v, g, beta):
    f32 = jnp.float32
    q, k, v, g, beta = (to_chunks(t.astype(f32)) for t in (q, k, v, g, beta))
    dk = q.shape[-1]
    q = q * dk ** -0.5
    gc = jnp.cumsum(g, axis=-1)
    causal = jnp.tril(jnp.ones((CHUNK, CHUNK), dtype=bool))
    strict = jnp.tril(jnp.ones((CHUNK, CHUNK), dtype=bool), k=-1)
    decay = jnp.exp(jnp.where(causal, gc[..., :, None] - gc[..., None, :], -jnp.inf))
    kb = k * beta[..., None]
    lower = jnp.where(strict, jnp.einsum('bhnid,bhnjd->bhnij', kb, k) * decay, 0.0)
    eye = jnp.eye(CHUNK, dtype=f32)
    t_inv = lax.linalg.triangular_solve(eye + lower, jnp.broadcast_to(eye, lower.shape),
                                        left_side=True, lower=True)
    u = jnp.einsum('bhnij,bhnje->bhnie', t_inv, v * beta[..., None])
    w = jnp.einsum('bhnij,bhnjd->bhnid', t_inv, kb * jnp.exp(gc)[..., None])
    attn = jnp.einsum('bhnid,bhnjd->bhnij', q, k) * decay
    qg = q * jnp.exp(gc)[..., None]
    kd = k * jnp.exp(gc[..., -1:] - gc)[..., None]
    dl = jnp.exp(gc[..., -1])

    def step(state, inp):
        qg_n, kd_n, u_n, w_n, a_n, dl_n = inp
        v_new = u_n - jnp.einsum('bhcd,bhde->bhce', w_n, state)
        o_n = jnp.einsum('bhcd,bhde->bhce', qg_n, state) + jnp.einsum('bhij,bhje->bhie', a_n, v_new)
        state = state * dl_n[..., None, None] + jnp.einsum('bhcd,bhce->bhde', kd_n, v_new)
        return state, o_n

    xs = tuple(jnp.moveaxis(t, 2, 0) for t in (qg, kd, u, w, attn, dl))
    state0 = jnp.zeros(q.shape[:2] + (dk, v.shape[-1]), f32)
    _, o = lax.scan(step, state0, xs)
    return from_chunks(jnp.moveaxis(o, 0, 2))


def gla_chunked(q, k, v, log_a):
    f32 = jnp.float32
    q, k, v, log_a = (to_chunks(t.astype(f32)) for t in (q, k, v, log_a))
    dk = q.shape[-1]
    q = q * dk ** -0.5
    bcum = jnp.cumsum(log_a, axis=-2)
    causal = jnp.tril(jnp.ones((CHUNK, CHUNK), dtype=bool))
    qe = q * jnp.exp(bcum)
    ke = k * jnp.exp(-bcum)
    attn = jnp.where(causal, jnp.einsum('bhnid,bhnjd->bhnij', qe, ke), 0.0)
    intra = jnp.einsum('bhnij,bhnje->bhnie', attn, v)
    kd = k * jnp.exp(bcum[..., -1:, :] - bcum)
    dl = jnp.exp(bcum[..., -1, :])

    def step(state, inp):
        qe_n, kd_n, v_n, dl_n = inp
        o_n = jnp.einsum('bhcd,bhde->bhce', qe_n, state)
        state = state * dl_n[..., :, None] + jnp.einsum('bhcd,bhce->bhde', kd_n, v_n)
        return state, o_n

    xs = tuple(jnp.moveaxis(t, 2, 0) for t in (qe, kd, v, dl))
    state0 = jnp.zeros(q.shape[:2] + (dk, v.shape[-1]), f32)
    _, inter = lax.scan(step, state0, xs)
    return from_chunks(intra + jnp.moveaxis(inter, 0, 2))


def ssd_chunked(x, dt, a, bm, cm):
    f32 = jnp.float32
    bsz, length, h, p = x.shape
    g = bm.shape[2]
    r = h // g
    n = length // CHUNK
    xc = to_chunks(x.astype(f32)).reshape(bsz, g, r, n, CHUNK, p)
    dtc = to_chunks(dt.astype(f32)).reshape(bsz, g, r, n, CHUNK)
    bc = to_chunks(bm.astype(f32))
    cc = to_chunks(cm.astype(f32))
    cs = jnp.cumsum(dtc * a.astype(f32).reshape(g, r)[None, :, :, None, None], axis=-1)
    xdt = xc * dtc[..., None]
    causal = jnp.tril(jnp.ones((CHUNK, CHUNK), dtype=bool))
    decay = jnp.exp(jnp.where(causal, cs[..., :, None] - cs[..., None, :], -jnp.inf))
    scores = jnp.einsum('bgnis,bgnjs->bgnij', cc, bc)
    y_intra = jnp.einsum('bgnij,bgrnij,bgrnjp->bgrnip', scores, decay, xdt)
    chunk_state = jnp.einsum('bgrnj,bgrnjp,bgnjs->bgrnps', jnp.exp(cs[..., -1:] - cs), xdt, bc)
    ecs = jnp.exp(cs)
    dl = jnp.exp(cs[..., -1])

    def step(state, inp):
        c_n, e_n, dl_n, st_n = inp
        y_n = jnp.einsum('bgis,bgrps->bgrip', c_n, state) * e_n[..., None]
        state = state * dl_n[..., None, None] + st_n
        return state, y_n

    xs = (jnp.moveaxis(cc, 2, 0), jnp.moveaxis(ecs, 3, 0), jnp.moveaxis(dl, 3, 0),
          jnp.moveaxis(chunk_state, 3, 0))
    state0 = jnp.zeros((bsz, g, r, p, bc.shape[-1]), f32)
    _, y_inter = lax.scan(step, state0, xs)
    y = y_intra + jnp.moveaxis(y_inter, 0, 3)
    return from_chunks(y.reshape(bsz, h, n, CHUNK, p))


def hybrid_mixer(x, w_in, conv_a_w, a_log_a, dt_bias_a, norm_a, w_lr_b, b_lr_b, norm_b,
                 conv_c_w, conv_c_b, a_log_c, dt_bias_c, d_c, norm_c, w_oa, w_ob, w_oc, w_out):
    f32 = jnp.float32
    bsz, length, _ = x.shape
    proj = x @ w_in
    offsets = [int(o) for o in np.cumsum(IN_SPLITS)[:-1]]
    (qa, ka, va, za, beta_a, alpha_a, qb, kb, vb, rb, lrb, zc, xbc, dtc, gates) = jnp.split(proj, offsets, axis=-1)

    qkv = jax.nn.silu(causal_dwconv(jnp.concatenate([qa, ka, va], axis=-1), conv_a_w))
    qa, ka, va = jnp.split(qkv, [A_QK, 2 * A_QK], axis=-1)
    qa = l2norm(qa.astype(f32).reshape(bsz, length, A_HEADS, A_DK))
    ka = l2norm(ka.astype(f32).reshape(bsz, length, A_HEADS, A_DK))
    va = va.reshape(bsz, length, A_HEADS, A_DV)
    beta = jax.nn.sigmoid(beta_a.astype(f32))
    g_a = -jnp.exp(a_log_a.astype(f32)) * jax.nn.softplus(alpha_a.astype(f32) + dt_bias_a.astype(f32))
    oa = gated_delta_chunked(qa, ka, va, g_a, beta)
    oa = rmsnorm(oa, norm_a) * jax.nn.silu(za.astype(f32).reshape(bsz, length, A_HEADS, A_DV))
    ya = oa.reshape(bsz, length, A_VW).astype(x.dtype) @ w_oa

    log_ab = jax.nn.log_sigmoid((lrb @ w_lr_b + b_lr_b).astype(f32)) / B_GATE_NORM
    ob = gla_chunked(qb.reshape(bsz, length, B_HEADS, B_DK), kb.reshape(bsz, length, B_HEADS, B_DK),
                     vb.reshape(bsz, length, B_HEADS, B_DV), log_ab.reshape(bsz, length, B_HEADS, B_DK))
    ob = rmsnorm(ob, norm_b) * jax.nn.silu(rb.astype(f32).reshape(bsz, length, B_HEADS, B_DV))
    yb = ob.reshape(bsz, length, B_VW).astype(x.dtype) @ w_ob

    xbc = jax.nn.silu(causal_dwconv(xbc, conv_c_w, conv_c_b))
    xs_c, bm, cm = jnp.split(xbc, [C_DINNER, C_DINNER + C_GROUPS * C_DSTATE], axis=-1)
    dt = jax.nn.softplus(dtc.astype(f32) + dt_bias_c.astype(f32))
    xs_c = xs_c.reshape(bsz, length, C_HEADS, C_HEADDIM)
    yc = ssd_chunked(xs_c, dt, -jnp.exp(a_log_c.astype(f32)),
                     bm.reshape(bsz, length, C_GROUPS, C_DSTATE), cm.reshape(bsz, length, C_GROUPS, C_DSTATE))
    yc = yc + xs_c.astype(f32) * d_c.astype(f32)[:, None]
    yz = yc.reshape(bsz, length, C_DINNER) * jax.nn.silu(zc.astype(f32))
    yz = rmsnorm(yz.reshape(bsz, length, C_GROUPS, C_DINNER // C_GROUPS), norm_c.reshape(C_GROUPS, -1))
    yc = yz.reshape(bsz, length, C_DINNER).astype(x.dtype) @ w_oc

    ga, gb, gcg = jnp.split(jax.nn.sigmoid(gates), N_BRANCH, axis=-1)
    merged = ga * ya + gb * yb + gcg * yc
    return merged @ w_out


def hier_moe(x, w_rg, w_re, w_gate_e, w_up_e, w_down_e):
    f32 = jnp.float32
    bsz, length, d = x.shape
    t = x.reshape(-1, d)
    p_group = jax.nn.softmax((t @ w_rg).astype(f32), axis=-1)
    p_g, g_sel = lax.top_k(p_group, 1)
    logits_e = (t @ w_re).astype(f32).reshape(-1, N_GROUPS, EXPERTS_PER_GROUP)
    le = jnp.take_along_axis(logits_e, g_sel[:, :, None], axis=1)[:, 0]
    top_p, top_i = lax.top_k(jax.nn.softmax(le, axis=-1), TOP_K)
    top_p = top_p / jnp.sum(top_p, axis=-1, keepdims=True)
    weights = p_g * top_p
    expert_id = g_sel * EXPERTS_PER_GROUP + top_i
    combine = jnp.sum(jax.nn.one_hot(expert_id, N_EXPERTS, dtype=f32) * weights[..., None], axis=1)
    y = jnp.zeros(t.shape, f32)
    for e in range(N_EXPERTS):
        hdn = jax.nn.silu(t @ w_gate_e[e]) * (t @ w_up_e[e])
        y = y + combine[:, e:e + 1] * (hdn @ w_down_e[e]).astype(f32)
    return y.astype(x.dtype).reshape(bsz, length, d)


def setup_inputs(seed: int = 0) -> dict:
    key = jax.random.key(seed)
    ks = jax.random.split(key, 32)
    f32 = jnp.float32

    def nrm(k, shape, scale):
        return jax.random.normal(k, shape, f32) * scale

    def a_log(k, n):
        return jnp.log(jax.random.uniform(k, (DEPTH, n), f32, 1.0, 16.0))

    def dt_bias(k, n):
        dt = jnp.exp(jax.random.uniform(k, (DEPTH, n), f32, np.log(1e-3), np.log(1e-1)))
        return dt + jnp.log(-jnp.expm1(-dt))

    def gain(k, n):
        return 1.0 + nrm(k, (DEPTH, n), 0.02)

    return {
        'x': nrm(ks[0], (BATCH, SEQ, D_MODEL), 1.0),
        'w_in': nrm(ks[1], (DEPTH, D_MODEL, N_IN), D_MODEL ** -0.5),
        'conv_a_w': nrm(ks[2], (DEPTH, CONV_W, 2 * A_QK + A_VW), CONV_W ** -0.5),
        'a_log_a': a_log(ks[3], A_HEADS),
        'dt_bias_a': dt_bias(ks[4], A_HEADS),
        'norm_a': gain(ks[5], A_DV),
        'w_lr_b': nrm(ks[6], (DEPTH, B_GATE_RANK, B_QK), B_GATE_RANK ** -0.5),
        'b_lr_b': nrm(ks[7], (DEPTH, B_QK), 0.02),
        'norm_b': gain(ks[8], B_DV),
        'conv_c_w': nrm(ks[9], (DEPTH, CONV_W, C_XBC), CONV_W ** -0.5),
        'conv_c_b': nrm(ks[10], (DEPTH, C_XBC), 0.02),
        'a_log_c': a_log(ks[11], C_HEADS),
        'dt_bias_c': dt_bias(ks[12], C_HEADS),
        'd_c': gain(ks[13], C_HEADS),
        'norm_c': gain(ks[14], C_DINNER),
        'w_oa': nrm(ks[15], (DEPTH, A_VW, D_MODEL), A_VW ** -0.5),
        'w_ob': nrm(ks[16], (DEPTH, B_VW, D_MODEL), B_VW ** -0.5),
        'w_oc': nrm(ks[17], (DEPTH, C_DINNER, D_MODEL), C_DINNER ** -0.5),
        'w_out': nrm(ks[18], (DEPTH, D_MODEL, D_MODEL), D_MODEL ** -0.5 * BETA),
        'ln1_g': gain(ks[19], D_MODEL),
        'ln1_b': nrm(ks[20], (DEPTH, D_MODEL), 0.02),
        'w_rg': nrm(ks[21], (DEPTH, D_MODEL, N_GROUPS), D_MODEL ** -0.5),
        'w_re': nrm(ks[22], (DEPTH, D_MODEL, N_EXPERTS), D_MODEL ** -0.5),
        'w_gate_e': nrm(ks[23], (DEPTH, N_EXPERTS, D_MODEL, D_EXPERT), D_MODEL ** -0.5),
        'w_up_e': nrm(ks[24], (DEPTH, N_EXPERTS, D_MODEL, D_EXPERT), D_MODEL ** -0.5),
        'w_down_e': nrm(ks[25], (DEPTH, N_EXPERTS, D_EXPERT, D_MODEL), D_EXPERT ** -0.5 * BETA),
        'ln2_g': gain(ks[26], D_MODEL),
        'ln2_b': nrm(ks[27], (DEPTH, D_MODEL), 0.02),
    }


def reference(x, w_in, conv_a_w, a_log_a, dt_bias_a, norm_a, w_lr_b, b_lr_b, norm_b,
              conv_c_w, conv_c_b, a_log_c, dt_bias_c, d_c, norm_c, w_oa, w_ob, w_oc, w_out,
              ln1_g, ln1_b, w_rg, w_re, w_gate_e, w_up_e, w_down_e, ln2_g, ln2_b):
    for l in range(DEPTH):
        mix = hybrid_mixer(x, w_in[l], conv_a_w[l], a_log_a[l], dt_bias_a[l], norm_a[l],
                           w_lr_b[l], b_lr_b[l], norm_b[l], conv_c_w[l], conv_c_b[l],
                           a_log_c[l], dt_bias_c[l], d_c[l], norm_c[l],
                           w_oa[l], w_ob[l], w_oc[l], w_out[l])
        x = layernorm(ALPHA * x + mix, ln1_g[l], ln1_b[l])
        ffn = hier_moe(x, w_rg[l], w_re[l], w_gate_e[l], w_up_e[l], w_down_e[l])
        x = layernorm(ALPHA * x + ffn, ln2_g[l], ln2_b[l])
    return x
```

```python
import functools

import jax
import jax.numpy as jnp
import numpy as np
from jax import lax
from jax.experimental import pallas as pl
from jax.experimental.pallas import tpu as pltpu

F32 = jnp.float32
BF16 = jnp.bfloat16

D_MODEL = 2048
DEPTH = 2
CHUNK = 64
CHUNK_SHIFT = 6
CONV_W = 4
A_HEADS, A_DK, A_DV = 8, 128, 128
B_HEADS, B_DK, B_DV = 4, 128, 256
B_GATE_RANK = 16
B_GATE_NORM = 16.0
C_HEADS, C_HEADDIM, C_GROUPS, C_DSTATE = 16, 64, 2, 128
C_DINNER = C_HEADS * C_HEADDIM
C_GROUP_W = C_DINNER // C_GROUPS
C_HEADS_PER_GROUP = C_HEADS // C_GROUPS
N_GROUPS, EXPERTS_PER_GROUP = 4, 4
N_EXPERTS = N_GROUPS * EXPERTS_PER_GROUP
D_EXPERT = 768
LN_EPS = 1e-5
RMS_EPS = 1e-6
ALPHA = (2 * DEPTH) ** 0.25

_QA, _KA, _VA, _ZA, _BETA, _ALPHA_A = 0, 1024, 2048, 3072, 4096, 4104
_QB, _KB, _VB, _RB, _LRB = 4112, 4624, 5136, 6160, 7184
_ZC, _XBC, _DTC, _GATES = 7200, 8224, 9760, 9776
P_GATES, P_QA, P_KA, P_VA, P_ZA = 0, 6144, 7168, 8192, 9216
P_QB, P_KB, P_VB, P_RB = 10240, 10752, 11264, 12288
P_ZC, P_XC, P_BC, P_CC = 13312, 14336, 15360, 15616
P_WIDTH = 15872
S_BETA, S_ALPHA, S_LRB, S_DT = 0, 8, 16, 32
LANES = 128

VMEM_LIMIT_BYTES = 56 * 1024 * 1024

SUPER = 256
A_HEADS_PER_STEP = 2
B_HEADS_PER_STEP = 2
PROJ_TM, PROJ_TN = 2048, 512
POST_TM = 256
MOE_TM = 512
DISPATCH_TM = 256
COMBINE_TM = 256


def _cparams(sem):
    return pltpu.CompilerParams(dimension_semantics=sem, vmem_limit_bytes=VMEM_LIMIT_BYTES)


def _dot(a, b):
    return jnp.dot(a, b, preferred_element_type=F32)


def _dot_nt(a, b):
    return lax.dot_general(a, b, (((1,), (1,)), ((), ())), preferred_element_type=F32)


def _dot_tn(a, b):
    return lax.dot_general(a, b, (((0,), (0,)), ((), ())), preferred_element_type=F32)


def _split3(x):
    hi = x.astype(BF16)
    r = x - hi.astype(F32)
    mid = r.astype(BF16)
    lo = (r - mid.astype(F32)).astype(BF16)
    return hi, mid, lo


def _dot_sel(sel_bf16, x):
    hi, mid, lo = _split3(x)
    return _dot(sel_bf16, hi) + _dot(sel_bf16, mid) + _dot(sel_bf16, lo)


def _x_dot_sel(x, sel_bf16):
    hi, mid, lo = _split3(x)
    return _dot(hi, sel_bf16) + _dot(mid, sel_bf16) + _dot(lo, sel_bf16)


def _silu(x):
    return x * jax.nn.sigmoid(x)


def _softplus(x):
    return jnp.maximum(x, 0.0) + jnp.log1p(jnp.exp(-jnp.abs(x)))


def _lane_col(arr, lane_iota, idx):
    return jnp.sum(jnp.where(lane_iota == idx, arr, 0.0), axis=1, keepdims=True)


def _chunk_masks(n):
    ri = lax.broadcasted_iota(jnp.int32, (n, n), 0)
    ci = lax.broadcasted_iota(jnp.int32, (n, n), 1)
    same = (ri >> CHUNK_SHIFT) == (ci >> CHUNK_SHIFT)
    return ri, ci, same, same & (ci <= ri)


def _cumsum_selectors(same, causal):
    return jnp.concatenate([jnp.where(causal, 1.0, 0.0), jnp.where(same, 1.0, 0.0)], axis=0).astype(BF16)


def _conv_silu(x_ref, w_ref, carry_ref, ext_ref, bias=None):
    n = x_ref.shape[0]
    x = x_ref[...].astype(F32)
    ext_ref[0:8, :] = carry_ref[...]
    ext_ref[8:8 + n, :] = x
    carry_ref[...] = x[n - 8:n, :]
    acc = w_ref[CONV_W - 1:CONV_W, :] * x
    for i in range(CONV_W - 1):
        off = 8 - (CONV_W - 1) + i
        acc = acc + w_ref[i:i + 1, :] * ext_ref[off:off + n, :]
    if bias is not None:
        acc = acc + bias
    return _silu(acc)


def _inproj_kernel(x_ref, wb_ref, ws_ref, ob_ref, os_ref):
    x = x_ref[...]
    ob_ref[...] = _dot(x, wb_ref[...]).astype(ob_ref.dtype)

    @pl.when(pl.program_id(1) == 0)
    def _():
        os_ref[...] = _dot(x, ws_ref[...])


def _inproj(xb, w_big, w_small):
    t = xb.shape[0]
    tm = min(PROJ_TM, t)
    assert t % tm == 0 and P_WIDTH % PROJ_TN == 0
    return pl.pallas_call(
        _inproj_kernel,
        grid=(t // tm, P_WIDTH // PROJ_TN),
        in_specs=[pl.BlockSpec((tm, D_MODEL), lambda i, j: (i, 0)),
                  pl.BlockSpec((D_MODEL, PROJ_TN), lambda i, j: (0, j)),
                  pl.BlockSpec((D_MODEL, LANES), lambda i, j: (0, 0))],
        out_specs=[pl.BlockSpec((tm, PROJ_TN), lambda i, j: (i, j)),
                   pl.BlockSpec((tm, LANES), lambda i, j: (i, 0))],
        out_shape=[jax.ShapeDtypeStruct((t, P_WIDTH), BF16), jax.ShapeDtypeStruct((t, LANES), F32)],
        compiler_params=_cparams(("parallel", "arbitrary")),
        name="inproj",
    )(xb, w_big, w_small)


def _mixa_kernel(q_ref, k_ref, v_ref, z_ref, sm_ref, cwq_ref, cwk_ref, cwv_ref, alog_ref, dtb_ref, gain_ref,
                 o_ref, s_ref, cq_ref, ck_ref, cv_ref, ext_ref, gct_ref, *, hb):
    hg = pl.program_id(1)
    n = q_ref.shape[0]

    @pl.when(pl.program_id(2) == 0)
    def _():
        s_ref[...] = jnp.zeros_like(s_ref)
        cq_ref[...] = jnp.zeros_like(cq_ref)
        ck_ref[...] = jnp.zeros_like(ck_ref)
        cv_ref[...] = jnp.zeros_like(cv_ref)

    q = _conv_silu(q_ref, cwq_ref, cq_ref, ext_ref)
    k = _conv_silu(k_ref, cwk_ref, ck_ref, ext_ref)
    v = _conv_silu(v_ref, cwv_ref, cv_ref, ext_ref)

    sm = sm_ref[...]
    lane = lax.broadcasted_iota(jnp.int32, (n, LANES), 1)
    beta_all = jax.nn.sigmoid(sm)
    g_all = -jnp.exp(alog_ref[...]) * _softplus(sm + dtb_ref[...])
    ri, ci, same, causal = _chunk_masks(n)
    strict = same & (ci < ri)
    gcc = _dot_sel(_cumsum_selectors(same, causal), g_all)
    gc, gl = gcc[:n], gcc[n:]
    egc, ekd, edl = jnp.exp(gc), jnp.exp(gl - gc), jnp.exp(gl)
    gct_ref[...] = gc.T
    lev = 31 - lax.clz(ri ^ ci)
    eye = jnp.where(ri == ci, 1.0, 0.0)

    for j in range(hb):
        h = hg * hb + j
        sl = slice(j * A_DK, (j + 1) * A_DK)
        qh, kh, vh = q[:, sl], k[:, sl], v[:, sl]
        qh = qh * lax.rsqrt(jnp.sum(qh * qh, axis=1, keepdims=True) + 1e-6) * (A_DK ** -0.5)
        kh = kh * lax.rsqrt(jnp.sum(kh * kh, axis=1, keepdims=True) + 1e-6)
        beta = _lane_col(beta_all, lane, S_BETA + h)
        gcol = _lane_col(gc, lane, S_ALPHA + h)
        e_gc = _lane_col(egc, lane, S_ALPHA + h)
        e_kd = _lane_col(ekd, lane, S_ALPHA + h)
        e_dl = _lane_col(edl, lane, S_ALPHA + h)
        grow = gct_ref[pl.ds(S_ALPHA + h, 1), :]
        dec = jnp.exp(jnp.where(causal, gcol - grow, -jnp.inf))
        kb = kh * beta
        kkqk = _dot_nt(jnp.concatenate([kb, qh], axis=0).astype(BF16), kh.astype(BF16))
        a = jnp.where(strict, kkqk[:n] * dec, 0.0)
        attn = kkqk[n:] * dec
        x = eye - jnp.where(lev == 0, a, 0.0)
        for ls in range(1, CHUNK_SHIFT):
            a_off = jnp.where(lev == ls, a, 0.0).astype(BF16)
            xb = x.astype(BF16)
            x = x - _dot(_dot(xb, a_off).astype(BF16), xb)
        uw = _dot(x.astype(BF16), jnp.concatenate([vh * beta, kb * e_gc], axis=1).astype(BF16))
        u, w = uw[:, :A_DV], uw[:, A_DV:]
        qg = qh * e_gc
        kd = kh * e_kd
        s = s_ref[j]
        v_new, o_inter = [], []
        for c in range(n // CHUNK):
            r = slice(c * CHUNK, (c + 1) * CHUNK)
            res = _dot(jnp.concatenate([w[r], qg[r]], axis=0).astype(BF16), s.astype(BF16))
            vn = u[r] - res[:CHUNK]
            o_inter.append(res[CHUNK:])
            v_new.append(vn)
            s = s * e_dl[c * CHUNK:c * CHUNK + 1, :] + _dot_tn(kd[r].astype(BF16), vn.astype(BF16))
        s_ref[j] = s
        o = jnp.concatenate(o_inter, axis=0) + _dot(attn.astype(BF16), jnp.concatenate(v_new, axis=0).astype(BF16))
        o = o * lax.rsqrt(jnp.mean(o * o, axis=1, keepdims=True) + RMS_EPS) * gain_ref[...]
        o_ref[:, sl] = (o * _silu(z_ref[:, sl].astype(F32))).astype(o_ref.dtype)


def _mixer_a(p, psm, cwq, cwk, cwv, alog, dtb, gain, bsz, length):
    hb = A_HEADS_PER_STEP
    w = hb * A_DK
    nl = length // SUPER

    def col(base):
        return lambda b, h, l: (b * nl + l, base // w + h)

    vec = lambda b, h, l: (0, 0)
    cw = lambda b, h, l: (0, h)
    return pl.pallas_call(
        functools.partial(_mixa_kernel, hb=hb),
        grid=(bsz, A_HEADS // hb, nl),
        in_specs=[pl.BlockSpec((SUPER, w), col(P_QA)), pl.BlockSpec((SUPER, w), col(P_KA)),
                  pl.BlockSpec((SUPER, w), col(P_VA)), pl.BlockSpec((SUPER, w), col(P_ZA)),
                  pl.BlockSpec((SUPER, LANES), lambda b, h, l: (b * nl + l, 0)),
                  pl.BlockSpec((CONV_W, w), cw), pl.BlockSpec((CONV_W, w), cw), pl.BlockSpec((CONV_W, w), cw),
                  pl.BlockSpec((1, LANES), vec), pl.BlockSpec((1, LANES), vec), pl.BlockSpec((1, A_DV), vec)],
        out_specs=pl.BlockSpec((SUPER, w), lambda b, h, l: (b * nl + l, h)),
        out_shape=jax.ShapeDtypeStruct((bsz * length, A_HEADS * A_DV), BF16),
        scratch_shapes=[pltpu.VMEM((hb, A_DK, A_DV), F32),
                        pltpu.VMEM((8, w), F32), pltpu.VMEM((8, w), F32), pltpu.VMEM((8, w), F32),
                        pltpu.VMEM((SUPER + 8, w), F32), pltpu.VMEM((LANES, SUPER), F32)],
        compiler_params=_cparams(("parallel", "parallel", "arbitrary")),
        name="mixer_a",
    )(p, p, p, p, psm, cwq, cwk, cwv, alog, dtb, gain)


def _mixb_kernel(q_ref, k_ref, v_ref, r_ref, sm_ref, wlr_ref, blr_ref, gain_ref, o_ref, st_ref, *, hb):
    n = q_ref.shape[0]

    @pl.when(pl.program_id(2) == 0)
    def _():
        st_ref[...] = jnp.zeros_like(st_ref)

    zl = _dot(sm_ref[...].astype(BF16), wlr_ref[...]) + blr_ref[...]
    log_a = -_softplus(-zl) * (1.0 / B_GATE_NORM)
    _, _, same, causal = _chunk_masks(n)
    bb = _dot_sel(_cumsum_selectors(same, causal), log_a)
    bc, bl = bb[:n], bb[n:]
    q = q_ref[...].astype(F32) * (B_DK ** -0.5)
    k = k_ref[...].astype(F32)
    qe = (q * jnp.exp(bc)).astype(BF16)
    ke = (k * jnp.exp(-bc)).astype(BF16)
    kd = (k * jnp.exp(bl - bc)).astype(BF16)
    edl = jnp.exp(bl)

    for j in range(hb):
        sl = slice(j * B_DK, (j + 1) * B_DK)
        sv = slice(j * B_DV, (j + 1) * B_DV)
        attn = jnp.where(causal, _dot_nt(qe[:, sl], ke[:, sl]), 0.0)
        vh = v_ref[:, sv]
        intra = _dot(attn.astype(BF16), vh)
        st = st_ref[j]
        inter = []
        for c in range(n // CHUNK):
            r = slice(c * CHUNK, (c + 1) * CHUNK)
            inter.append(_dot_nt(qe[r, sl], st.astype(BF16)))
            st = st * edl[c * CHUNK:c * CHUNK + 1, sl] + _dot_tn(vh[r], kd[r, sl])
        st_ref[j] = st
        o = intra + jnp.concatenate(inter, axis=0)
        o = o * lax.rsqrt(jnp.mean(o * o, axis=1, keepdims=True) + RMS_EPS) * gain_ref[...]
        o_ref[:, sv] = (o * _silu(r_ref[:, sv].astype(F32))).astype(o_ref.dtype)


def _mixer_b(p, psm, wlr, blr, gain, bsz, length):
    hb = B_HEADS_PER_STEP
    wk, wv = hb * B_DK, hb * B_DV
    nl = length // SUPER

    def col(base, width):
        return lambda b, h, l: (b * nl + l, base // width + h)

    return pl.pallas_call(
        functools.partial(_mixb_kernel, hb=hb),
        grid=(bsz, B_HEADS // hb, nl),
        in_specs=[pl.BlockSpec((SUPER, wk), col(P_QB, wk)), pl.BlockSpec((SUPER, wk), col(P_KB, wk)),
                  pl.BlockSpec((SUPER, wv), col(P_VB, wv)), pl.BlockSpec((SUPER, wv), col(P_RB, wv)),
                  pl.BlockSpec((SUPER, LANES), lambda b, h, l: (b * nl + l, 0)),
                  pl.BlockSpec((LANES, wk), lambda b, h, l: (0, h)),
                  pl.BlockSpec((1, wk), lambda b, h, l: (0, h)),
                  pl.BlockSpec((1, B_DV), lambda b, h, l: (0, 0))],
        out_specs=pl.BlockSpec((SUPER, wv), lambda b, h, l: (b * nl + l, h)),
        out_shape=jax.ShapeDtypeStruct((bsz * length, B_HEADS * B_DV), BF16),
        scratch_shapes=[pltpu.VMEM((hb, B_DV, B_DK), F32)],
        compiler_params=_cparams(("parallel", "parallel", "arbitrary")),
        name="mixer_b",
    )(p, p, p, p, psm, wlr, blr, gain)


def _mixc_kernel(z_ref, x_ref, b_ref, c_ref, sm_ref, cwx_ref, cbx_ref, cwb_ref, cbb_ref, cwc_ref, cbc_ref,
                 alog_ref, dtb_ref, dskip_ref, gain_ref, o_ref,
                 st_ref, cx_ref, cb_ref, cc_ref, extx_ref, extb_ref, cst_ref):
    g = pl.program_id(1)
    n = x_ref.shape[0]

    @pl.when(pl.program_id(2) == 0)
    def _():
        st_ref[...] = jnp.zeros_like(st_ref)
        cx_ref[...] = jnp.zeros_like(cx_ref)
        cb_ref[...] = jnp.zeros_like(cb_ref)
        cc_ref[...] = jnp.zeros_like(cc_ref)

    xs = _conv_silu(x_ref, cwx_ref, cx_ref, extx_ref, cbx_ref[...])
    bm = _conv_silu(b_ref, cwb_ref, cb_ref, extb_ref, cbb_ref[...]).astype(BF16)
    cm = _conv_silu(c_ref, cwc_ref, cc_ref, extb_ref, cbc_ref[...]).astype(BF16)

    lane = lax.broadcasted_iota(jnp.int32, (n, LANES), 1)
    dt_all = _softplus(sm_ref[...] + dtb_ref[...])
    da = dt_all * (-jnp.exp(alog_ref[...]))
    _, _, same, causal = _chunk_masks(n)
    cc = _dot_sel(_cumsum_selectors(same, causal), da)
    cs, csl = cc[:n], cc[n:]
    cst_ref[...] = cs.T
    er = lax.broadcasted_iota(jnp.int32, (LANES, C_GROUP_W), 0)
    ec = lax.broadcasted_iota(jnp.int32, (LANES, C_GROUP_W), 1)
    head0 = S_DT + g * C_HEADS_PER_GROUP
    expand = jnp.where(er == head0 + ec // C_HEADDIM, 1.0, 0.0).astype(BF16)
    ex = _x_dot_sel(jnp.concatenate([dt_all, cs, csl], axis=0), expand)
    dt_e, cs_e, csl_e = ex[:n], ex[n:2 * n], ex[2 * n:]
    ecs, edl = jnp.exp(cs_e), jnp.exp(csl_e)
    xdt = xs * dt_e
    xdt_b = xdt.astype(BF16)
    xdtw = (xdt * jnp.exp(csl_e - cs_e)).astype(BF16)
    scores = jnp.where(causal, _dot_nt(cm, bm), 0.0)

    half = lax.broadcasted_iota(jnp.int32, (n, LANES), 1) // C_HEADDIM
    y_intra = []
    for pair in range(C_HEADS_PER_GROUP // 2):
        xp = xdt_b[:, pair * LANES:(pair + 1) * LANES]
        acc = None
        for sub in range(2):
            hidx = head0 + 2 * pair + sub
            ccol = _lane_col(cs, lane, hidx)
            crow = cst_ref[pl.ds(hidx, 1), :]
            dec = jnp.exp(jnp.where(causal, ccol - crow, -jnp.inf))
            part = _dot((scores * dec).astype(BF16), jnp.where(half == sub, xp, jnp.zeros_like(xp)))
            acc = part if acc is None else acc + part
        y_intra.append(acc)
    y = jnp.concatenate(y_intra, axis=1)

    st = st_ref[...]
    y_inter = []
    for c in range(n // CHUNK):
        r = slice(c * CHUNK, (c + 1) * CHUNK)
        y_inter.append(_dot(cm[r], st.astype(BF16)) * ecs[r])
        st = st * edl[c * CHUNK:c * CHUNK + 1, :] + _dot_tn(bm[r], xdtw[r])
    st_ref[...] = st
    y = y + jnp.concatenate(y_inter, axis=0) + xs * dskip_ref[...]
    yz = y * _silu(z_ref[...].astype(F32))
    yz = yz * lax.rsqrt(jnp.mean(yz * yz, axis=1, keepdims=True) + RMS_EPS) * gain_ref[...]
    o_ref[...] = yz.astype(o_ref.dtype)


def _mixer_c(p, psm, cwx, cbx, cwb, cbb, cwc, cbc, alog, dtb, dskip, gain, bsz, length):
    nl = length // SUPER
    gw, ds = C_GROUP_W, C_DSTATE

    def col(base, width):
        return lambda b, g, l: (b * nl + l, base // width + g)

    grp = lambda b, g, l: (0, g)
    vec = lambda b, g, l: (0, 0)
    return pl.pallas_call(
        _mixc_kernel,
        grid=(bsz, C_GROUPS, nl),
        in_specs=[pl.BlockSpec((SUPER, gw), col(P_ZC, gw)), pl.BlockSpec((SUPER, gw), col(P_XC, gw)),
                  pl.BlockSpec((SUPER, ds), col(P_BC, ds)), pl.BlockSpec((SUPER, ds), col(P_CC, ds)),
                  pl.BlockSpec((SUPER, LANES), lambda b, g, l: (b * nl + l, 0)),
                  pl.BlockSpec((CONV_W, gw), grp), pl.BlockSpec((1, gw), grp),
                  pl.BlockSpec((CONV_W, ds), grp), pl.BlockSpec((1, ds), grp),
                  pl.BlockSpec((CONV_W, ds), grp), pl.BlockSpec((1, ds), grp),
                  pl.BlockSpec((1, LANES), vec), pl.BlockSpec((1, LANES), vec),
                  pl.BlockSpec((1, gw), grp), pl.BlockSpec((1, gw), grp)],
        out_specs=pl.BlockSpec((SUPER, gw), lambda b, g, l: (b * nl + l, g)),
        out_shape=jax.ShapeDtypeStruct((bsz * length, C_DINNER), BF16),
        scratch_shapes=[pltpu.VMEM((ds, gw), F32),
                        pltpu.VMEM((8, gw), F32), pltpu.VMEM((8, ds), F32), pltpu.VMEM((8, ds), F32),
                        pltpu.VMEM((SUPER + 8, gw), F32), pltpu.VMEM((SUPER + 8, ds), F32),
                        pltpu.VMEM((LANES, SUPER), F32)],
        compiler_params=_cparams(("parallel", "parallel", "arbitrary")),
        name="mixer_c",
    )(p, p, p, p, psm, cwx, cbx, cwb, cbb, cwc, cbc, alog, dtb, dskip, gain)


def _layernorm(y, g, b):
    mu = jnp.mean(y, axis=1, keepdims=True)
    yc = y - mu
    var = jnp.mean(yc * yc, axis=1, keepdims=True)
    return yc * lax.rsqrt(var + LN_EPS) * g + b


def _post_kernel(oa_ref, ob_ref, oc_ref, ga_ref, gb_ref, gc_ref, x_ref, woa_ref, wob_ref, woc_ref, wout_ref,
                 wrh_ref, wrl_ref, lng_ref, lnb_ref, x1_ref, ridx_ref, rw_ref, cnt_ref, carry_ref):
    tm = x_ref.shape[0]

    @pl.when(pl.program_id(0) == 0)
    def _():
        carry_ref[...] = jnp.zeros_like(carry_ref)

    merged = jax.nn.sigmoid(ga_ref[...].astype(F32)) * _dot(oa_ref[...], woa_ref[...])
    merged = merged + jax.nn.sigmoid(gb_ref[...].astype(F32)) * _dot(ob_ref[...], wob_ref[...])
    merged = merged + jax.nn.sigmoid(gc_ref[...].astype(F32)) * _dot(oc_ref[...], woc_ref[...])
    y = ALPHA * x_ref[...] + _dot(merged.astype(BF16), wout_ref[...])
    x1 = _layernorm(y, lng_ref[...], lnb_ref[...])
    x1_ref[...] = x1

    xh = x1.astype(BF16)
    xl = (x1 - xh.astype(F32)).astype(BF16)
    logits = _dot(xh, wrh_ref[...]) + _dot(xl, wrh_ref[...]) + _dot(xh, wrl_ref[...])
    lane = lax.broadcasted_iota(jnp.int32, (tm, LANES), 1)
    neg = -jnp.inf
    is_g = lane < N_GROUPS
    gl = jnp.where(is_g, logits, neg)
    gmax = jnp.max(gl, axis=1, keepdims=True)
    gsum = jnp.sum(jnp.where(is_g, jnp.exp(logits - gmax), 0.0), axis=1, keepdims=True)
    p_g = 1.0 / gsum
    g_sel = jnp.min(jnp.where(gl == gmax, lane, LANES), axis=1, keepdims=True)
    in_g = (lane >= N_GROUPS) & (lane < N_GROUPS + N_EXPERTS) & (((lane - N_GROUPS) >> 2) == g_sel)
    emax = jnp.max(jnp.where(in_g, logits, neg), axis=1, keepdims=True)
    ee = jnp.where(in_g, jnp.exp(logits - emax), 0.0)
    probs = jnp.where(in_g, ee / jnp.sum(ee, axis=1, keepdims=True), -1.0)
    p1 = jnp.max(probs, axis=1, keepdims=True)
    i1 = jnp.min(jnp.where(probs == p1, lane, LANES), axis=1, keepdims=True)
    probs2 = jnp.where(lane == i1, -1.0, probs)
    p2 = jnp.max(probs2, axis=1, keepdims=True)
    i2 = jnp.min(jnp.where(probs2 == p2, lane, LANES), axis=1, keepdims=True)
    psum = p1 + p2
    w1 = p_g * (p1 / psum)
    w2 = p_g * (p2 / psum)
    e1 = i1 - N_GROUPS
    e2 = i2 - N_GROUPS

    oh1 = lane == e1
    oh2 = lane == e2
    cnt = jnp.where(oh1 | oh2, 1.0, 0.0)
    ri = lax.broadcasted_iota(jnp.int32, (tm, tm), 0)
    ci = lax.broadcasted_iota(jnp.int32, (tm, tm), 1)
    before = _dot(jnp.where(ci < ri, 1.0, 0.0).astype(BF16), cnt.astype(BF16)) + carry_ref[0:1, :]
    r1 = jnp.sum(jnp.where(oh1, before, 0.0), axis=1, keepdims=True).astype(jnp.int32)
    r2 = jnp.sum(jnp.where(oh2, before, 0.0), axis=1, keepdims=True).astype(jnp.int32)
    total = carry_ref[0:1, :] + jnp.sum(cnt, axis=0, keepdims=True)
    carry_ref[...] = jnp.broadcast_to(total, carry_ref.shape)
    cnt_ref[...] = jnp.broadcast_to(total, cnt_ref.shape)
    zero = jnp.zeros_like(lane)
    ridx_ref[...] = jnp.where(lane == 0, e1, jnp.where(lane == 1, e2,
                              jnp.where(lane == 2, r1, jnp.where(lane == 3, r2, zero))))
    rw_ref[...] = jnp.where(lane == 0, w1, jnp.where(lane == 1, w2, 0.0))


def _post(oa, ob, oc, p, x, woa, wob, woc, wout, wrh, wrl, lng, lnb):
    t = x.shape[0]
    tm = min(POST_TM, t)
    d = D_MODEL
    row = lambda i: (i, 0)
    const = lambda i: (0, 0)
    one = pl.Buffered(1)
    wspec = lambda shape: pl.BlockSpec(shape, const, pipeline_mode=one)
    return pl.pallas_call(
        _post_kernel,
        grid=(t // tm,),
        in_specs=[pl.BlockSpec((tm, 1024), row), pl.BlockSpec((tm, 1024), row), pl.BlockSpec((tm, 1024), row),
                  pl.BlockSpec((tm, d), lambda i: (i, P_GATES // d)),
                  pl.BlockSpec((tm, d), lambda i: (i, P_GATES // d + 1)),
                  pl.BlockSpec((tm, d), lambda i: (i, P_GATES // d + 2)),
                  pl.BlockSpec((tm, d), row),
                  wspec((1024, d)), wspec((1024, d)), wspec((1024, d)), wspec((d, d)),
                  wspec((d, LANES)), wspec((d, LANES)), wspec((1, d)), wspec((1, d))],
        out_specs=[pl.BlockSpec((tm, d), row), pl.BlockSpec((tm, LANES), row), pl.BlockSpec((tm, LANES), row),
                   pl.BlockSpec((8, LANES), const)],
        out_shape=[jax.ShapeDtypeStruct((t, d), F32), jax.ShapeDtypeStruct((t, LANES), jnp.int32),
                   jax.ShapeDtypeStruct((t, LANES), F32), jax.ShapeDtypeStruct((8, LANES), F32)],
        scratch_shapes=[pltpu.VMEM((8, LANES), F32)],
        compiler_params=_cparams(("arbitrary",)),
        name="merge_ln_router",
    )(oa, ob, oc, p, p, p, x, woa, wob, woc, wout, wrh, wrl, lng, lnb)


def _row_copy(src_ref, src_row, dst_ref, dst_row, sem):
    return pltpu.make_async_copy(src_ref.at[pl.ds(src_row, 1), :], dst_ref.at[pl.ds(dst_row, 1), :], sem)


def _dispatch_kernel(slot_ref, x_ref, zeros_ref, xs_ref, sem):
    del zeros_ref
    tm = x_ref.shape[0]
    base = pl.program_id(0) * tm

    def start(r, carry):
        for kk in range(2):
            _row_copy(x_ref, r, xs_ref, slot_ref[2 * (base + r) + kk], sem).start()
        return carry

    def wait(r, carry):
        for kk in range(2):
            _row_copy(x_ref, r, xs_ref, slot_ref[2 * (base + r) + kk], sem).wait()
        return carry

    lax.fori_loop(0, tm, start, 0)
    lax.fori_loop(0, tm, wait, 0)


def _dispatch(slot, x1, n_slots):
    t = x1.shape[0]
    tm = min(DISPATCH_TM, t)
    zeros = jnp.zeros((n_slots, D_MODEL), F32)
    return pl.pallas_call(
        _dispatch_kernel,
        grid_spec=pltpu.PrefetchScalarGridSpec(
            num_scalar_prefetch=1, grid=(t // tm,),
            in_specs=[pl.BlockSpec((tm, D_MODEL), lambda i, s: (i, 0)), pl.BlockSpec(memory_space=pl.ANY)],
            out_specs=pl.BlockSpec(memory_space=pl.ANY),
            scratch_shapes=[pltpu.SemaphoreType.DMA(())]),
        out_shape=jax.ShapeDtypeStruct((n_slots, D_MODEL), F32),
        input_output_aliases={2: 0},
        compiler_params=_cparams(("arbitrary",)),
        name="moe_dispatch",
    )(slot, x1, zeros)


def _expert_kernel(blk_ref, exp_ref, nused_ref, x_ref, wg_ref, wu_ref, wd_ref, y_ref):
    del blk_ref, exp_ref

    used = pl.program_id(0) < nused_ref[0]

    @pl.when(used)
    def _():
        xb = x_ref[...].astype(BF16)
        hid = _silu(_dot(xb, wg_ref[0])) * _dot(xb, wu_ref[0])
        y_ref[...] = _dot(hid.astype(BF16), wd_ref[0])

    @pl.when(jnp.logical_not(used))
    def _():
        y_ref[...] = jnp.zeros_like(y_ref)


def _experts(tile_blk, tile_exp, n_used, xs, wg, wu, wd):
    n_slots = xs.shape[0]
    tm = MOE_TM
    d, de = D_MODEL, D_EXPERT
    return pl.pallas_call(
        _expert_kernel,
        grid_spec=pltpu.PrefetchScalarGridSpec(
            num_scalar_prefetch=3, grid=(n_slots // tm,),
            in_specs=[pl.BlockSpec((tm, d), lambda i, tb, te, nu: (tb[i], 0)),
                      pl.BlockSpec((1, d, de), lambda i, tb, te, nu: (te[i], 0, 0)),
                      pl.BlockSpec((1, d, de), lambda i, tb, te, nu: (te[i], 0, 0)),
                      pl.BlockSpec((1, de, d), lambda i, tb, te, nu: (te[i], 0, 0))],
            out_specs=pl.BlockSpec((tm, d), lambda i, tb, te, nu: (i, 0))),
        out_shape=jax.ShapeDtypeStruct((n_slots, d), F32),
        compiler_params=_cparams(("arbitrary",)),
        name="moe_experts",
    )(tile_blk, tile_exp, n_used, xs, wg, wu, wd)


def _combine_kernel(slot_ref, ys_ref, x1_ref, rw_ref, lng_ref, lnb_ref, x2_ref, x2b_ref, ybuf_ref, sem):
    tm = x1_ref.shape[0]
    base = pl.program_id(0) * tm

    def start(r, carry):
        for kk in range(2):
            _row_copy(ys_ref, slot_ref[2 * (base + r) + kk], ybuf_ref.at[kk], r, sem).start()
        return carry

    def wait(r, carry):
        for kk in range(2):
            _row_copy(ys_ref, slot_ref[2 * (base + r) + kk], ybuf_ref.at[kk], r, sem).wait()
        return carry

    lax.fori_loop(0, tm, start, 0)
    lax.fori_loop(0, tm, wait, 0)
    rw = rw_ref[...]
    y = rw[:, 0:1] * ybuf_ref[0] + rw[:, 1:2] * ybuf_ref[1]
    x2 = _layernorm(ALPHA * x1_ref[...] + y, lng_ref[...], lnb_ref[...])
    x2_ref[...] = x2
    x2b_ref[...] = x2.astype(BF16)


def _combine(slot, ys, x1, rw, lng, lnb):
    t = x1.shape[0]
    tm = min(COMBINE_TM, t)
    d = D_MODEL
    return pl.pallas_call(
        _combine_kernel,
        grid_spec=pltpu.PrefetchScalarGridSpec(
            num_scalar_prefetch=1, grid=(t // tm,),
            in_specs=[pl.BlockSpec(memory_space=pl.ANY),
                      pl.BlockSpec((tm, d), lambda i, s: (i, 0)),
                      pl.BlockSpec((tm, LANES), lambda i, s: (i, 0)),
                      pl.BlockSpec((1, d), lambda i, s: (0, 0)), pl.BlockSpec((1, d), lambda i, s: (0, 0))],
            out_specs=[pl.BlockSpec((tm, d), lambda i, s: (i, 0)), pl.BlockSpec((tm, d), lambda i, s: (i, 0))],
            scratch_shapes=[pltpu.VMEM((2, tm, d), F32), pltpu.SemaphoreType.DMA(())]),
        out_shape=[jax.ShapeDtypeStruct((t, d), F32), jax.ShapeDtypeStruct((t, d), BF16)],
        compiler_params=_cparams(("arbitrary",)),
        name="moe_combine_ln",
    )(slot, ys, x1, rw, lng, lnb)


def _routing_tables(ridx, cnt, t):
    tm = MOE_TM
    n_tiles = (2 * t) // tm + N_EXPERTS
    counts = cnt[0, :N_EXPERTS].astype(jnp.int32)
    padded = ((counts + tm - 1) // tm) * tm
    ends = jnp.cumsum(padded)
    offs = ends - padded
    slot = (offs[ridx[:, 0:2]] + ridx[:, 2:4]).reshape(-1)
    n_used = ends[-1] // tm
    tile_blk = jnp.minimum(jnp.arange(n_tiles, dtype=jnp.int32), n_used - 1)
    tile_exp = jnp.minimum(jnp.sum(tile_blk[:, None] * tm >= ends[None, :], axis=1), N_EXPERTS - 1).astype(jnp.int32)
    return slot.astype(jnp.int32), tile_blk, tile_exp, n_used.reshape(1).astype(jnp.int32), n_tiles * tm


def _lanes(vec, start):
    return jnp.zeros((1, LANES), F32).at[0, start:start + vec.shape[0]].set(vec.astype(F32))


def _layer_params(w_in, conv_a_w, a_log_a, dt_bias_a, norm_a, w_lr_b, b_lr_b, norm_b, conv_c_w, conv_c_b,
                  a_log_c, dt_bias_c, d_c, norm_c, w_oa, w_ob, w_oc, w_out, ln1_g, ln1_b, w_rg, w_re,
                  w_gate_e, w_up_e, w_down_e, ln2_g, ln2_b):
    def cols(a, n):
        return w_in[:, a:a + n]

    w_big = jnp.concatenate([
        cols(_GATES, 3 * D_MODEL), cols(_QA, 1024), cols(_KA, 1024), cols(_VA, 1024), cols(_ZA, 1024),
        cols(_QB, 512), cols(_KB, 512), cols(_VB, 1024), cols(_RB, 1024),
        cols(_ZC, 1024), cols(_XBC, 1536)], axis=1).astype(BF16)
    w_small = jnp.concatenate([cols(_BETA, 8), cols(_ALPHA_A, 8), cols(_LRB, 16), cols(_DTC, 16),
                               jnp.zeros((D_MODEL, LANES - 48), F32)], axis=1).astype(BF16)
    w_r = jnp.concatenate([w_rg, w_re, jnp.zeros((D_MODEL, LANES - N_GROUPS - N_EXPERTS), F32)], axis=1)
    w_rh = w_r.astype(BF16)
    w_rl = (w_r - w_rh.astype(F32)).astype(BF16)
    wlr = jnp.zeros((LANES, B_HEADS * B_DK), F32).at[S_LRB:S_LRB + B_GATE_RANK].set(w_lr_b).astype(BF16)
    row = lambda v: v.reshape(1, -1).astype(F32)
    return dict(
        w_big=w_big, w_small=w_small,
        cwq=conv_a_w[:, :1024], cwk=conv_a_w[:, 1024:2048], cwv=conv_a_w[:, 2048:],
        alog_a=_lanes(a_log_a, S_ALPHA), dtb_a=_lanes(dt_bias_a, S_ALPHA), gain_a=row(norm_a),
        wlr=wlr, blr=row(b_lr_b), gain_b=row(norm_b),
        cwx=conv_c_w[:, :C_DINNER], cbx=row(conv_c_b[:C_DINNER]),
        cwb=conv_c_w[:, C_DINNER:C_DINNER + 256], cbb=row(conv_c_b[C_DINNER:C_DINNER + 256]),
        cwc=conv_c_w[:, C_DINNER + 256:], cbc=row(conv_c_b[C_DINNER + 256:]),
        alog_c=_lanes(a_log_c, S_DT), dtb_c=_lanes(dt_bias_c, S_DT),
        dskip=row(jnp.repeat(d_c, C_HEADDIM)), gain_c=row(norm_c),
        woa=w_oa.astype(BF16), wob=w_ob.astype(BF16), woc=w_oc.astype(BF16), wout=w_out.astype(BF16),
        w_rh=w_rh, w_rl=w_rl, ln1_g=row(ln1_g), ln1_b=row(ln1_b),
        wg=w_gate_e.astype(BF16), wu=w_up_e.astype(BF16), wd=w_down_e.astype(BF16),
        ln2_g=row(ln2_g), ln2_b=row(ln2_b))


def _layer(x, xb, pr, bsz, length):
    t = bsz * length
    p, psm = _inproj(xb, pr["w_big"], pr["w_small"])
    oa = _mixer_a(p, psm, pr["cwq"], pr["cwk"], pr["cwv"], pr["alog_a"], pr["dtb_a"], pr["gain_a"], bsz, length)
    ob = _mixer_b(p, psm, pr["wlr"], pr["blr"], pr["gain_b"], bsz, length)
    oc = _mixer_c(p, psm, pr["cwx"], pr["cbx"], pr["cwb"], pr["cbb"], pr["cwc"], pr["cbc"],
                  pr["alog_c"], pr["dtb_c"], pr["dskip"], pr["gain_c"], bsz, length)
    x1, ridx, rw, cnt = _post(oa, ob, oc, p, x, pr["woa"], pr["wob"], pr["woc"], pr["wout"],
                              pr["w_rh"], pr["w_rl"], pr["ln1_g"], pr["ln1_b"])
    slot, tile_blk, tile_exp, n_used, n_slots = _routing_tables(ridx, cnt, t)
    xs = _dispatch(slot, x1, n_slots)
    ys = _experts(tile_blk, tile_exp, n_used, xs, pr["wg"], pr["wu"], pr["wd"])
    return _combine(slot, ys, x1, rw, pr["ln2_g"], pr["ln2_b"])


def kernel(x, w_in, conv_a_w, a_log_a, dt_bias_a, norm_a, w_lr_b, b_lr_b, norm_b, conv_c_w, conv_c_b, a_log_c,
           dt_bias_c, d_c, norm_c, w_oa, w_ob, w_oc, w_out, ln1_g, ln1_b, w_rg, w_re, w_gate_e, w_up_e,
           w_down_e, ln2_g, ln2_b):
    bsz, length, d = x.shape
    assert d == D_MODEL and length % SUPER == 0
    params = (w_in, conv_a_w, a_log_a, dt_bias_a, norm_a, w_lr_b, b_lr_b, norm_b, conv_c_w, conv_c_b, a_log_c,
              dt_bias_c, d_c, norm_c, w_oa, w_ob, w_oc, w_out, ln1_g, ln1_b, w_rg, w_re, w_gate_e, w_up_e,
              w_down_e, ln2_g, ln2_b)
    xf = x.reshape(bsz * length, d)
    xb = xf.astype(BF16)
    for layer in range(params[0].shape[0]):
        pr = _layer_params(*(p[layer] for p in params))
        xf, xb = _layer(xf, xb, pr, bsz, length)
    return xf.reshape(bsz, length, d)
```

```python
import functools

import jax
import jax.numpy as jnp
import numpy as np
from jax import lax
from jax.experimental import pallas as pl
from jax.experimental.pallas import tpu as pltpu

F32 = jnp.float32
BF16 = jnp.bfloat16

D_MODEL = 2048
DEPTH = 2
CHUNK = 64
CHUNK_SHIFT = 6
CONV_W = 4
A_HEADS, A_DK, A_DV = 8, 128, 128
B_HEADS, B_DK, B_DV = 4, 128, 256
B_GATE_RANK = 16
B_GATE_NORM = 16.0
C_HEADS, C_HEADDIM, C_GROUPS, C_DSTATE = 16, 64, 2, 128
C_DINNER = C_HEADS * C_HEADDIM
C_GROUP_W = C_DINNER // C_GROUPS
C_HEADS_PER_GROUP = C_HEADS // C_GROUPS
N_GROUPS, EXPERTS_PER_GROUP = 4, 4
N_EXPERTS = N_GROUPS * EXPERTS_PER_GROUP
D_EXPERT = 768
PAIR_A = (0, 0, 0, 1, 1, 3)
PAIR_B = (1, 2, 3, 3, 2, 2)
N_PAIRS = len(PAIR_A)
N_SEG = N_GROUPS * N_PAIRS
LN_EPS = 1e-5
RMS_EPS = 1e-6
ALPHA = (2 * DEPTH) ** 0.25

_QA, _KA, _VA, _ZA, _BETA, _ALPHA_A = 0, 1024, 2048, 3072, 4096, 4104
_QB, _KB, _VB, _RB, _LRB = 4112, 4624, 5136, 6160, 7184
_ZC, _XBC, _DTC, _GATES = 7200, 8224, 9760, 9776
P_GATES, P_QA, P_KA, P_VA, P_ZA = 0, 6144, 7168, 8192, 9216
P_QB, P_KB, P_VB, P_RB = 10240, 10752, 11264, 12288
P_ZC, P_XC, P_BC, P_CC = 13312, 14336, 15360, 15616
P_WIDTH = 15872
S_BETA, S_ALPHA, S_LRB, S_DT = 0, 8, 16, 32
LANES = 128
XE_W = D_MODEL + LANES
ZERO_ROWS = 64

VMEM_LIMIT_BYTES = 56 * 1024 * 1024

SUPER = 256
A_HEADS_PER_STEP = 4
B_HEADS_PER_STEP = 4
PROJ_TM, PROJ_TN = 2048, 512
POST_TM = 256
MOE_TM = 512
DISPATCH_TM = 256
COMBINE_TM = 256


def _cparams(sem):
    return pltpu.CompilerParams(dimension_semantics=sem, vmem_limit_bytes=VMEM_LIMIT_BYTES)


def _dot(a, b):
    return jnp.dot(a, b, preferred_element_type=F32)


def _dot_nt(a, b):
    return lax.dot_general(a, b, (((1,), (1,)), ((), ())), preferred_element_type=F32)


def _dot_tn(a, b):
    return lax.dot_general(a, b, (((0,), (0,)), ((), ())), preferred_element_type=F32)


def _split3(x):
    hi = x.astype(BF16)
    r = x - hi.astype(F32)
    mid = r.astype(BF16)
    lo = (r - mid.astype(F32)).astype(BF16)
    return hi, mid, lo


def _dot_sel(sel_bf16, x):
    hi, mid, lo = _split3(x)
    return _dot(sel_bf16, hi) + _dot(sel_bf16, mid) + _dot(sel_bf16, lo)


def _x_dot_sel(x, sel_bf16):
    hi, mid, lo = _split3(x)
    return _dot(hi, sel_bf16) + _dot(mid, sel_bf16) + _dot(lo, sel_bf16)


def _silu(x):
    return x * jax.nn.sigmoid(x)


def _softplus(x):
    return jnp.maximum(x, 0.0) + jnp.log1p(jnp.exp(-jnp.abs(x)))


def _lane_col(arr, lane_iota, idx):
    return jnp.sum(jnp.where(lane_iota == idx, arr, 0.0), axis=1, keepdims=True)


def _chunk_masks(n):
    ri = lax.broadcasted_iota(jnp.int32, (n, n), 0)
    ci = lax.broadcasted_iota(jnp.int32, (n, n), 1)
    same = (ri >> CHUNK_SHIFT) == (ci >> CHUNK_SHIFT)
    return ri, ci, same, same & (ci <= ri)


def _cumsum_selectors(same, causal):
    return jnp.concatenate([jnp.where(causal, 1.0, 0.0), jnp.where(same, 1.0, 0.0)], axis=0).astype(BF16)


def _conv_silu(x_ref, w_ref, carry_ref, ext_ref, bias=None):
    n = x_ref.shape[0]
    x = x_ref[...].astype(F32)
    ext_ref[0:8, :] = carry_ref[...]
    ext_ref[8:8 + n, :] = x
    carry_ref[...] = x[n - 8:n, :]
    acc = w_ref[CONV_W - 1:CONV_W, :] * x
    for i in range(CONV_W - 1):
        off = 8 - (CONV_W - 1) + i
        acc = acc + w_ref[i:i + 1, :] * ext_ref[off:off + n, :]
    if bias is not None:
        acc = acc + bias
    return _silu(acc)


def _inproj_kernel(x_ref, wb_ref, ws_ref, ob_ref, os_ref):
    x = x_ref[...]
    ob_ref[...] = _dot(x, wb_ref[...]).astype(ob_ref.dtype)

    @pl.when(pl.program_id(1) == 0)
    def _():
        os_ref[...] = _dot(x, ws_ref[...])


def _inproj(xb, w_big, w_small):
    t = xb.shape[0]
    tm = min(PROJ_TM, t)
    assert t % tm == 0 and P_WIDTH % PROJ_TN == 0
    return pl.pallas_call(
        _inproj_kernel,
        grid=(t // tm, P_WIDTH // PROJ_TN),
        in_specs=[pl.BlockSpec((tm, D_MODEL), lambda i, j: (i, 0)),
                  pl.BlockSpec((D_MODEL, PROJ_TN), lambda i, j: (0, j)),
                  pl.BlockSpec((D_MODEL, LANES), lambda i, j: (0, 0))],
        out_specs=[pl.BlockSpec((tm, PROJ_TN), lambda i, j: (i, j)),
                   pl.BlockSpec((tm, LANES), lambda i, j: (i, 0))],
        out_shape=[jax.ShapeDtypeStruct((t, P_WIDTH), BF16), jax.ShapeDtypeStruct((t, LANES), F32)],
        compiler_params=_cparams(("parallel", "arbitrary")),
        name="inproj",
    )(xb, w_big, w_small)


def _mixa_kernel(q_ref, k_ref, v_ref, z_ref, sm_ref, cwq_ref, cwk_ref, cwv_ref, alog_ref, dtb_ref, gain_ref,
                 o_ref, s_ref, cq_ref, ck_ref, cv_ref, ext_ref, gct_ref, *, hb):
    hg = pl.program_id(1)
    n = q_ref.shape[0]

    @pl.when(pl.program_id(2) == 0)
    def _():
        s_ref[...] = jnp.zeros_like(s_ref)
        cq_ref[...] = jnp.zeros_like(cq_ref)
        ck_ref[...] = jnp.zeros_like(ck_ref)
        cv_ref[...] = jnp.zeros_like(cv_ref)

    q = _conv_silu(q_ref, cwq_ref, cq_ref, ext_ref)
    k = _conv_silu(k_ref, cwk_ref, ck_ref, ext_ref)
    v = _conv_silu(v_ref, cwv_ref, cv_ref, ext_ref)

    sm = sm_ref[...]
    lane = lax.broadcasted_iota(jnp.int32, (n, LANES), 1)
    beta_all = jax.nn.sigmoid(sm)
    g_all = -jnp.exp(alog_ref[...]) * _softplus(sm + dtb_ref[...])
    ri, ci, same, causal = _chunk_masks(n)
    strict = same & (ci < ri)
    gcc = _dot_sel(_cumsum_selectors(same, causal), g_all)
    gc, gl = gcc[:n], gcc[n:]
    egc, ekd, edl = jnp.exp(gc), jnp.exp(gl - gc), jnp.exp(gl)
    gct_ref[...] = gc.T
    lev = 31 - lax.clz(ri ^ ci)
    eye = jnp.where(ri == ci, 1.0, 0.0)

    heads = range(hb)
    sls = [slice(j * A_DK, (j + 1) * A_DK) for j in heads]
    e_dls, a_s, attns, rhs, wqs, kds = [], [], [], [], [], []
    for j in heads:
        h = hg * hb + j
        qh, kh, vh = q[:, sls[j]], k[:, sls[j]], v[:, sls[j]]
        qh = qh * lax.rsqrt(jnp.sum(qh * qh, axis=1, keepdims=True) + 1e-6) * (A_DK ** -0.5)
        kh = kh * lax.rsqrt(jnp.sum(kh * kh, axis=1, keepdims=True) + 1e-6)
        beta = _lane_col(beta_all, lane, S_BETA + h)
        gcol = _lane_col(gc, lane, S_ALPHA + h)
        e_gc = _lane_col(egc, lane, S_ALPHA + h)
        e_kd = _lane_col(ekd, lane, S_ALPHA + h)
        e_dls.append(_lane_col(edl, lane, S_ALPHA + h))
        grow = gct_ref[pl.ds(S_ALPHA + h, 1), :]
        dec = jnp.exp(jnp.where(causal, gcol - grow, -jnp.inf))
        kb = kh * beta
        kkqk = _dot_nt(jnp.concatenate([kb, qh], axis=0).astype(BF16), kh.astype(BF16))
        a_s.append(jnp.where(strict, kkqk[:n] * dec, 0.0))
        attns.append((kkqk[n:] * dec).astype(BF16))
        rhs.append(jnp.concatenate([vh * beta, kb * e_gc], axis=1).astype(BF16))
        wqs.append((qh * e_gc).astype(BF16))
        kds.append((kh * e_kd).astype(BF16))
    xs = [eye - jnp.where(lev == 0, a, 0.0) for a in a_s]
    for ls in range(1, CHUNK_SHIFT):
        level = lev == ls
        xbs = [x.astype(BF16) for x in xs]
        ts = [_dot(xb, jnp.where(level, a, 0.0).astype(BF16)).astype(BF16) for xb, a in zip(xbs, a_s)]
        xs = [x - _dot(t, xb) for x, t, xb in zip(xs, ts, xbs)]
    uws = [_dot(x.astype(BF16), r) for x, r in zip(xs, rhs)]
    ws = [uw[:, A_DV:].astype(BF16) for uw in uws]
    ss = [s_ref[j] for j in heads]
    v_new = [[] for _ in heads]
    o_inter = [[] for _ in heads]
    for c in range(n // CHUNK):
        r = slice(c * CHUNK, (c + 1) * CHUNK)
        res = [_dot(jnp.concatenate([ws[j][r], wqs[j][r]], axis=0), ss[j].astype(BF16)) for j in heads]
        for j in heads:
            o_inter[j].append(res[j][CHUNK:])
            v_new[j].append((uws[j][r, :A_DV] - res[j][:CHUNK]).astype(BF16))
        ss = [ss[j] * e_dls[j][c * CHUNK:c * CHUNK + 1, :] + _dot_tn(kds[j][r], v_new[j][c]) for j in heads]
    for j in heads:
        s_ref[j] = ss[j]
        o = jnp.concatenate(o_inter[j], axis=0) + _dot(attns[j], jnp.concatenate(v_new[j], axis=0))
        o = o * lax.rsqrt(jnp.mean(o * o, axis=1, keepdims=True) + RMS_EPS) * gain_ref[...]
        o_ref[:, sls[j]] = (o * _silu(z_ref[:, sls[j]].astype(F32))).astype(o_ref.dtype)


def _mixer_a(p, psm, cwq, cwk, cwv, alog, dtb, gain, bsz, length):
    hb = A_HEADS_PER_STEP
    w = hb * A_DK
    nl = length // SUPER

    def col(base):
        return lambda b, h, l: (b * nl + l, base // w + h)

    vec = lambda b, h, l: (0, 0)
    cw = lambda b, h, l: (0, h)
    return pl.pallas_call(
        functools.partial(_mixa_kernel, hb=hb),
        grid=(bsz, A_HEADS // hb, nl),
        in_specs=[pl.BlockSpec((SUPER, w), col(P_QA)), pl.BlockSpec((SUPER, w), col(P_KA)),
                  pl.BlockSpec((SUPER, w), col(P_VA)), pl.BlockSpec((SUPER, w), col(P_ZA)),
                  pl.BlockSpec((SUPER, LANES), lambda b, h, l: (b * nl + l, 0)),
                  pl.BlockSpec((CONV_W, w), cw), pl.BlockSpec((CONV_W, w), cw), pl.BlockSpec((CONV_W, w), cw),
                  pl.BlockSpec((1, LANES), vec), pl.BlockSpec((1, LANES), vec), pl.BlockSpec((1, A_DV), vec)],
        out_specs=pl.BlockSpec((SUPER, w), lambda b, h, l: (b * nl + l, h)),
        out_shape=jax.ShapeDtypeStruct((bsz * length, A_HEADS * A_DV), BF16),
        scratch_shapes=[pltpu.VMEM((hb, A_DK, A_DV), F32),
                        pltpu.VMEM((8, w), F32), pltpu.VMEM((8, w), F32), pltpu.VMEM((8, w), F32),
                        pltpu.VMEM((SUPER + 8, w), F32), pltpu.VMEM((LANES, SUPER), F32)],
        compiler_params=_cparams(("parallel", "parallel", "arbitrary")),
        name="mixer_a",
    )(p, p, p, p, psm, cwq, cwk, cwv, alog, dtb, gain)


def _mixb_kernel(q_ref, k_ref, v_ref, r_ref, sm_ref, wlr_ref, blr_ref, gain_ref, o_ref, st_ref, *, hb):
    n = q_ref.shape[0]

    @pl.when(pl.program_id(2) == 0)
    def _():
        st_ref[...] = jnp.zeros_like(st_ref)

    zl = _dot(sm_ref[...].astype(BF16), wlr_ref[...]) + blr_ref[...]
    log_a = -_softplus(-zl) * (1.0 / B_GATE_NORM)
    _, _, same, causal = _chunk_masks(n)
    bb = _dot_sel(_cumsum_selectors(same, causal), log_a)
    bc, bl = bb[:n], bb[n:]
    q = q_ref[...].astype(F32) * (B_DK ** -0.5)
    k = k_ref[...].astype(F32)
    qe = (q * jnp.exp(bc)).astype(BF16)
    ke = (k * jnp.exp(-bc)).astype(BF16)
    kd = (k * jnp.exp(bl - bc)).astype(BF16)
    edl = jnp.exp(bl)

    heads = range(hb)
    chunks = range(n // CHUNK)
    rows = [slice(c * CHUNK, (c + 1) * CHUNK) for c in chunks]
    sls = [slice(j * B_DK, (j + 1) * B_DK) for j in heads]
    svs = [slice(j * B_DV, (j + 1) * B_DV) for j in heads]
    vhs = [v_ref[:, svs[j]] for j in heads]
    upd = [[_dot_tn(vhs[j][r], kd[r, sls[j]]) for r in rows] for j in heads]
    attn = [jnp.where(causal, _dot_nt(qe[:, sls[j]], ke[:, sls[j]]), 0.0).astype(BF16) for j in heads]
    intra = [_dot(attn[j], vhs[j]) for j in heads]
    for j in heads:
        st = st_ref[j]
        inter = []
        for c in chunks:
            inter.append(_dot_nt(qe[rows[c], sls[j]], st.astype(BF16)))
            st = st * edl[c * CHUNK:c * CHUNK + 1, sls[j]] + upd[j][c]
        st_ref[j] = st
        o = intra[j] + jnp.concatenate(inter, axis=0)
        o = o * lax.rsqrt(jnp.mean(o * o, axis=1, keepdims=True) + RMS_EPS) * gain_ref[...]
        o_ref[:, svs[j]] = (o * _silu(r_ref[:, svs[j]].astype(F32))).astype(o_ref.dtype)


def _mixer_b(p, psm, wlr, blr, gain, bsz, length):
    hb = B_HEADS_PER_STEP
    wk, wv = hb * B_DK, hb * B_DV
    nl = length // SUPER

    def col(base, width):
        return lambda b, h, l: (b * nl + l, base // width + h)

    return pl.pallas_call(
        functools.partial(_mixb_kernel, hb=hb),
        grid=(bsz, B_HEADS // hb, nl),
        in_specs=[pl.BlockSpec((SUPER, wk), col(P_QB, wk)), pl.BlockSpec((SUPER, wk), col(P_KB, wk)),
                  pl.BlockSpec((SUPER, wv), col(P_VB, wv)), pl.BlockSpec((SUPER, wv), col(P_RB, wv)),
                  pl.BlockSpec((SUPER, LANES), lambda b, h, l: (b * nl + l, 0)),
                  pl.BlockSpec((LANES, wk), lambda b, h, l: (0, h)),
                  pl.BlockSpec((1, wk), lambda b, h, l: (0, h)),
                  pl.BlockSpec((1, B_DV), lambda b, h, l: (0, 0))],
        out_specs=pl.BlockSpec((SUPER, wv), lambda b, h, l: (b * nl + l, h)),
        out_shape=jax.ShapeDtypeStruct((bsz * length, B_HEADS * B_DV), BF16),
        scratch_shapes=[pltpu.VMEM((hb, B_DV, B_DK), F32)],
        compiler_params=_cparams(("parallel", "parallel", "arbitrary")),
        name="mixer_b",
    )(p, p, p, p, psm, wlr, blr, gain)


def _mixc_kernel(z_ref, x_ref, b_ref, c_ref, sm_ref, cwx_ref, cbx_ref, cwb_ref, cbb_ref, cwc_ref, cbc_ref,
                 alog_ref, dtb_ref, dskip_ref, gain_ref, o_ref,
                 st_ref, cx_ref, cb_ref, cc_ref, extx_ref, extb_ref, cst_ref):
    g = pl.program_id(1)
    n = x_ref.shape[0]

    @pl.when(pl.program_id(2) == 0)
    def _():
        st_ref[...] = jnp.zeros_like(st_ref)
        cx_ref[...] = jnp.zeros_like(cx_ref)
        cb_ref[...] = jnp.zeros_like(cb_ref)
        cc_ref[...] = jnp.zeros_like(cc_ref)

    xs = _conv_silu(x_ref, cwx_ref, cx_ref, extx_ref, cbx_ref[...])
    bm = _conv_silu(b_ref, cwb_ref, cb_ref, extb_ref, cbb_ref[...]).astype(BF16)
    cm = _conv_silu(c_ref, cwc_ref, cc_ref, extb_ref, cbc_ref[...]).astype(BF16)

    lane = lax.broadcasted_iota(jnp.int32, (n, LANES), 1)
    dt_all = _softplus(sm_ref[...] + dtb_ref[...])
    da = dt_all * (-jnp.exp(alog_ref[...]))
    _, _, same, causal = _chunk_masks(n)
    cc = _dot_sel(_cumsum_selectors(same, causal), da)
    cs, csl = cc[:n], cc[n:]
    cst_ref[...] = cs.T
    er = lax.broadcasted_iota(jnp.int32, (LANES, C_GROUP_W), 0)
    ec = lax.broadcasted_iota(jnp.int32, (LANES, C_GROUP_W), 1)
    head0 = S_DT + g * C_HEADS_PER_GROUP
    expand = jnp.where(er == head0 + ec // C_HEADDIM, 1.0, 0.0).astype(BF16)
    ex = _x_dot_sel(jnp.concatenate([dt_all, cs, csl], axis=0), expand)
    dt_e, cs_e, csl_e = ex[:n], ex[n:2 * n], ex[2 * n:]
    ecs, edl = jnp.exp(cs_e), jnp.exp(csl_e)
    xdt = xs * dt_e
    xdt_b = xdt.astype(BF16)
    xdtw = (xdt * jnp.exp(csl_e - cs_e)).astype(BF16)
    scores = jnp.where(causal, _dot_nt(cm, bm), 0.0)

    half = lax.broadcasted_iota(jnp.int32, (n, LANES), 1) // C_HEADDIM
    y_intra = []
    for pair in range(C_HEADS_PER_GROUP // 2):
        xp = xdt_b[:, pair * LANES:(pair + 1) * LANES]
        acc = None
        for sub in range(2):
            hidx = head0 + 2 * pair + sub
            ccol = _lane_col(cs, lane, hidx)
            crow = cst_ref[pl.ds(hidx, 1), :]
            dec = jnp.exp(jnp.where(causal, ccol - crow, -jnp.inf))
            part = _dot((scores * dec).astype(BF16), jnp.where(half == sub, xp, jnp.zeros_like(xp)))
            acc = part if acc is None else acc + part
        y_intra.append(acc)
    y = jnp.concatenate(y_intra, axis=1)

    st = st_ref[...]
    rows = [slice(c * CHUNK, (c + 1) * CHUNK) for c in range(n // CHUNK)]
    upd = [_dot_tn(bm[r], xdtw[r]) for r in rows]
    y_inter = []
    for c, r in enumerate(rows):
        y_inter.append(_dot(cm[r], st.astype(BF16)) * ecs[r])
        st = st * edl[c * CHUNK:c * CHUNK + 1, :] + upd[c]
    st_ref[...] = st
    y = y + jnp.concatenate(y_inter, axis=0) + xs * dskip_ref[...]
    yz = y * _silu(z_ref[...].astype(F32))
    yz = yz * lax.rsqrt(jnp.mean(yz * yz, axis=1, keepdims=True) + RMS_EPS) * gain_ref[...]
    o_ref[...] = yz.astype(o_ref.dtype)


def _mixer_c(p, psm, cwx, cbx, cwb, cbb, cwc, cbc, alog, dtb, dskip, gain, bsz, length):
    nl = length // SUPER
    gw, ds = C_GROUP_W, C_DSTATE

    def col(base, width):
        return lambda b, g, l: (b * nl + l, base // width + g)

    grp = lambda b, g, l: (0, g)
    vec = lambda b, g, l: (0, 0)
    return pl.pallas_call(
        _mixc_kernel,
        grid=(bsz, C_GROUPS, nl),
        in_specs=[pl.BlockSpec((SUPER, gw), col(P_ZC, gw)), pl.BlockSpec((SUPER, gw), col(P_XC, gw)),
                  pl.BlockSpec((SUPER, ds), col(P_BC, ds)), pl.BlockSpec((SUPER, ds), col(P_CC, ds)),
                  pl.BlockSpec((SUPER, LANES), lambda b, g, l: (b * nl + l, 0)),
                  pl.BlockSpec((CONV_W, gw), grp), pl.BlockSpec((1, gw), grp),
                  pl.BlockSpec((CONV_W, ds), grp), pl.BlockSpec((1, ds), grp),
                  pl.BlockSpec((CONV_W, ds), grp), pl.BlockSpec((1, ds), grp),
                  pl.BlockSpec((1, LANES), vec), pl.BlockSpec((1, LANES), vec),
                  pl.BlockSpec((1, gw), grp), pl.BlockSpec((1, gw), grp)],
        out_specs=pl.BlockSpec((SUPER, gw), lambda b, g, l: (b * nl + l, g)),
        out_shape=jax.ShapeDtypeStruct((bsz * length, C_DINNER), BF16),
        scratch_shapes=[pltpu.VMEM((ds, gw), F32),
                        pltpu.VMEM((8, gw), F32), pltpu.VMEM((8, ds), F32), pltpu.VMEM((8, ds), F32),
                        pltpu.VMEM((SUPER + 8, gw), F32), pltpu.VMEM((SUPER + 8, ds), F32),
                        pltpu.VMEM((LANES, SUPER), F32)],
        compiler_params=_cparams(("parallel", "parallel", "arbitrary")),
        name="mixer_c",
    )(p, p, p, p, psm, cwx, cbx, cwb, cbb, cwc, cbc, alog, dtb, dskip, gain)


def _layernorm(y, g, b):
    mu = jnp.mean(y, axis=1, keepdims=True)
    yc = y - mu
    var = jnp.mean(yc * yc, axis=1, keepdims=True)
    return yc * lax.rsqrt(var + LN_EPS) * g + b


def _post_kernel(oa_ref, ob_ref, oc_ref, ga_ref, gb_ref, gc_ref, x_ref, woa_ref, wob_ref, woc_ref, wout_ref,
                 wrh_ref, wrl_ref, lng_ref, lnb_ref, x1e_ref, ridx_ref, cnt_ref, carry_ref):
    tm = x_ref.shape[0]

    @pl.when(pl.program_id(0) == 0)
    def _():
        carry_ref[...] = jnp.zeros_like(carry_ref)

    merged = jax.nn.sigmoid(ga_ref[...].astype(F32)) * _dot(oa_ref[...], woa_ref[...])
    merged = merged + jax.nn.sigmoid(gb_ref[...].astype(F32)) * _dot(ob_ref[...], wob_ref[...])
    merged = merged + jax.nn.sigmoid(gc_ref[...].astype(F32)) * _dot(oc_ref[...], woc_ref[...])
    y = ALPHA * x_ref[...] + _dot(merged.astype(BF16), wout_ref[...])
    x1 = _layernorm(y, lng_ref[...], lnb_ref[...])
    x1e_ref[:, :D_MODEL] = x1

    xh = x1.astype(BF16)
    xl = (x1 - xh.astype(F32)).astype(BF16)
    logits = _dot(xh, wrh_ref[...]) + _dot(xl, wrh_ref[...]) + _dot(xh, wrl_ref[...])
    lane = lax.broadcasted_iota(jnp.int32, (tm, LANES), 1)
    neg = -jnp.inf
    is_g = lane < N_GROUPS
    gl = jnp.where(is_g, logits, neg)
    gmax = jnp.max(gl, axis=1, keepdims=True)
    gsum = jnp.sum(jnp.where(is_g, jnp.exp(logits - gmax), 0.0), axis=1, keepdims=True)
    p_g = 1.0 / gsum
    g_sel = jnp.min(jnp.where(gl == gmax, lane, LANES), axis=1, keepdims=True)
    in_g = (lane >= N_GROUPS) & (lane < N_GROUPS + N_EXPERTS) & (((lane - N_GROUPS) >> 2) == g_sel)
    emax = jnp.max(jnp.where(in_g, logits, neg), axis=1, keepdims=True)
    ee = jnp.where(in_g, jnp.exp(logits - emax), 0.0)
    probs = jnp.where(in_g, ee / jnp.sum(ee, axis=1, keepdims=True), -1.0)
    p1 = jnp.max(probs, axis=1, keepdims=True)
    i1 = jnp.min(jnp.where(probs == p1, lane, LANES), axis=1, keepdims=True)
    probs2 = jnp.where(lane == i1, -1.0, probs)
    p2 = jnp.max(probs2, axis=1, keepdims=True)
    i2 = jnp.min(jnp.where(probs2 == p2, lane, LANES), axis=1, keepdims=True)
    psum = p1 + p2
    w1 = p_g * (p1 / psum)
    w2 = p_g * (p2 / psum)
    l1 = (i1 - N_GROUPS) & (EXPERTS_PER_GROUP - 1)
    l2 = (i2 - N_GROUPS) & (EXPERTS_PER_GROUP - 1)
    la = jnp.minimum(l1, l2)
    lb = jnp.maximum(l1, l2)
    pair = jnp.where(la == 0, lb - 1, jnp.where(la == 1, 6 - lb, 5))
    seg = g_sel * N_PAIRS + pair
    first_is_a = l1 == jnp.where(pair < 3, PAIR_A[0], jnp.where(pair < 5, PAIR_A[3], PAIR_A[5]))
    wa = jnp.where(first_is_a, w1, w2)
    wb = jnp.where(first_is_a, w2, w1)

    oh = lane == seg
    cnt = jnp.where(oh, 1.0, 0.0)
    ri = lax.broadcasted_iota(jnp.int32, (tm, tm), 0)
    ci = lax.broadcasted_iota(jnp.int32, (tm, tm), 1)
    before = _dot(jnp.where(ci < ri, 1.0, 0.0).astype(BF16), cnt.astype(BF16)) + carry_ref[0:1, :]
    rank = jnp.sum(jnp.where(oh, before, 0.0), axis=1, keepdims=True).astype(jnp.int32)
    total = carry_ref[0:1, :] + jnp.sum(cnt, axis=0, keepdims=True)
    carry_ref[...] = jnp.broadcast_to(total, carry_ref.shape)
    cnt_ref[...] = jnp.broadcast_to(total, cnt_ref.shape)
    ridx_ref[...] = jnp.where(lane == 0, seg, jnp.where(lane == 1, rank, jnp.zeros_like(lane)))
    x1e_ref[:, D_MODEL:] = jnp.where(lane == 0, wa, jnp.where(lane == 1, wb, 0.0))


def _post(oa, ob, oc, p, x, woa, wob, woc, wout, wrh, wrl, lng, lnb):
    t = x.shape[0]
    tm = min(POST_TM, t)
    d = D_MODEL
    row = lambda i: (i, 0)
    const = lambda i: (0, 0)
    one = pl.Buffered(1)
    wspec = lambda shape: pl.BlockSpec(shape, const, pipeline_mode=one)
    return pl.pallas_call(
        _post_kernel,
        grid=(t // tm,),
        in_specs=[pl.BlockSpec((tm, 1024), row), pl.BlockSpec((tm, 1024), row), pl.BlockSpec((tm, 1024), row),
                  pl.BlockSpec((tm, d), lambda i: (i, P_GATES // d)),
                  pl.BlockSpec((tm, d), lambda i: (i, P_GATES // d + 1)),
                  pl.BlockSpec((tm, d), lambda i: (i, P_GATES // d + 2)),
                  pl.BlockSpec((tm, d), row),
                  wspec((1024, d)), wspec((1024, d)), wspec((1024, d)), wspec((d, d)),
                  wspec((d, LANES)), wspec((d, LANES)), wspec((1, d)), wspec((1, d))],
        out_specs=[pl.BlockSpec((tm, XE_W), row), pl.BlockSpec((tm, LANES), row), pl.BlockSpec((8, LANES), const)],
        out_shape=[jax.ShapeDtypeStruct((t, XE_W), F32), jax.ShapeDtypeStruct((t, LANES), jnp.int32),
                   jax.ShapeDtypeStruct((8, LANES), F32)],
        scratch_shapes=[pltpu.VMEM((8, LANES), F32)],
        compiler_params=_cparams(("arbitrary",)),
        name="merge_ln_router",
    )(oa, ob, oc, p, p, p, x, woa, wob, woc, wout, wrh, wrl, lng, lnb)


def _row_copy(src_ref, src_row, dst_ref, dst_row, sem):
    return pltpu.make_async_copy(src_ref.at[pl.ds(src_row, 1), :], dst_ref.at[pl.ds(dst_row, 1), :], sem)


def _dispatch_kernel(slot_ref, pad_lo_ref, pad_hi_ref, x_ref, xs_ref, zbuf_ref, sem):
    tm = x_ref.shape[0]
    step = pl.program_id(0)
    base = step * tm

    @pl.when(step == 0)
    def _():
        zbuf_ref[...] = jnp.zeros_like(zbuf_ref)

        def zero_copy(lo, kk):
            return _row_copy(zbuf_ref, 0, xs_ref, lo + kk, sem)

        def bucket(sg, carry):
            lo = pad_lo_ref[sg]
            n_copies = pad_hi_ref[sg] - lo

            def start(kk, c):
                zero_copy(lo, kk).start()
                return c

            def wait(kk, c):
                zero_copy(lo, kk).wait()
                return c

            lax.fori_loop(0, n_copies, start, 0)
            lax.fori_loop(0, n_copies, wait, 0)
            return carry

        lax.fori_loop(0, N_SEG, bucket, 0)

        zr = zbuf_ref.shape[0]
        tail_lo = pad_hi_ref[N_SEG - 1]

        def tail_copy(kk):
            return pltpu.make_async_copy(zbuf_ref, xs_ref.at[pl.ds(pl.multiple_of(tail_lo + kk * zr, zr), zr), :], sem)

        def tail_start(kk, c):
            tail_copy(kk).start()
            return c

        def tail_wait(kk, c):
            tail_copy(kk).wait()
            return c

        n_tail = (xs_ref.shape[0] - tail_lo) // zr
        lax.fori_loop(0, n_tail, tail_start, 0)
        lax.fori_loop(0, n_tail, tail_wait, 0)

    def start_row(r, carry):
        _row_copy(x_ref, r, xs_ref, slot_ref[base + r], sem).start()
        return carry

    lax.fori_loop(0, tm, start_row, 0, unroll=8)
    pltpu.make_async_copy(x_ref, xs_ref.at[pl.ds(0, tm), :], sem).wait()


def _dispatch(slot, pad_lo, pad_hi, x1e, n_slots):
    t = x1e.shape[0]
    tm = min(DISPATCH_TM, t)
    return pl.pallas_call(
        _dispatch_kernel,
        grid_spec=pltpu.PrefetchScalarGridSpec(
            num_scalar_prefetch=3, grid=(t // tm,),
            in_specs=[pl.BlockSpec((tm, XE_W), lambda i, s, lo, hi: (i, 0))],
            out_specs=pl.BlockSpec(memory_space=pl.ANY),
            scratch_shapes=[pltpu.VMEM((ZERO_ROWS, XE_W), F32), pltpu.SemaphoreType.DMA(())]),
        out_shape=jax.ShapeDtypeStruct((n_slots, XE_W), F32),
        compiler_params=_cparams(("arbitrary",)),
        name="moe_dispatch",
    )(slot, pad_lo, pad_hi, x1e)


def _expert_kernel(blk_ref, ea_ref, eb_ref, valid_ref, nused_ref, x_ref, wga_ref, wua_ref, wda_ref,
                   wgb_ref, wub_ref, wdb_ref, y_ref):
    del blk_ref, ea_ref, eb_ref
    tm = x_ref.shape[0]
    half = tm // 2
    step = pl.program_id(0)
    used = step < nused_ref[0]
    full = valid_ref[step] > half

    def run(rows):
        x = x_ref[0:rows, :]
        xb = x[:, :D_MODEL].astype(BF16)
        ha = (_silu(_dot(xb, wga_ref[0])) * _dot(xb, wua_ref[0])).astype(BF16)
        hb = (_silu(_dot(xb, wgb_ref[0])) * _dot(xb, wub_ref[0])).astype(BF16)
        wa = x[:, D_MODEL:D_MODEL + 1]
        wb = x[:, D_MODEL + 1:D_MODEL + 2]
        y_ref[0:rows, :] = wa * _dot(ha, wda_ref[0]) + wb * _dot(hb, wdb_ref[0])

    @pl.when(used & full)
    def _():
        run(tm)

    @pl.when(used & jnp.logical_not(full))
    def _():
        run(half)
        y_ref[half:tm, :] = jnp.zeros((tm - half, D_MODEL), F32)

    @pl.when(jnp.logical_not(used))
    def _():
        y_ref[...] = jnp.zeros_like(y_ref)


def _experts(tile_blk, tile_ea, tile_eb, tile_valid, n_used, xs, wg, wu, wd):
    tm = MOE_TM
    n_tiles = tile_blk.shape[0]
    d, de = D_MODEL, D_EXPERT
    one = pl.Buffered(1)

    def wspec(shape, which):
        return pl.BlockSpec(shape, lambda i, tb, ea, eb, nv, nu: ((ea, eb)[which][i], 0, 0), pipeline_mode=one)

    return pl.pallas_call(
        _expert_kernel,
        grid_spec=pltpu.PrefetchScalarGridSpec(
            num_scalar_prefetch=5, grid=(n_tiles,),
            in_specs=[pl.BlockSpec((tm, XE_W), lambda i, tb, ea, eb, nv, nu: (tb[i], 0)),
                      wspec((1, d, de), 0), wspec((1, d, de), 0), wspec((1, de, d), 0),
                      wspec((1, d, de), 1), wspec((1, d, de), 1), wspec((1, de, d), 1)],
            out_specs=pl.BlockSpec((tm, d), lambda i, tb, ea, eb, nv, nu: (i, 0))),
        out_shape=jax.ShapeDtypeStruct((n_tiles * tm, d), F32),
        compiler_params=_cparams(("arbitrary",)),
        name="moe_experts",
    )(tile_blk, tile_ea, tile_eb, tile_valid, n_used, xs, wg, wu, wd, wg, wu, wd)


def _combine_kernel(slot_ref, ys_ref, x1_ref, lng_ref, lnb_ref, x2_ref, x2b_ref, ybuf_ref, sem):
    tm = x1_ref.shape[0]
    step = pl.program_id(0)

    def gather(s):
        buf = s % 2

        def start_row(r, carry):
            _row_copy(ys_ref, slot_ref[s * tm + r], ybuf_ref.at[buf], r, sem.at[buf]).start()
            return carry

        lax.fori_loop(0, tm, start_row, 0, unroll=8)

    @pl.when(step == 0)
    def _():
        gather(step)

    @pl.when(step + 1 < pl.num_programs(0))
    def _():
        gather(step + 1)

    buf = step % 2
    pltpu.make_async_copy(ys_ref.at[pl.ds(0, tm), :], ybuf_ref.at[buf], sem.at[buf]).wait()
    x2 = _layernorm(ALPHA * x1_ref[...] + ybuf_ref[buf], lng_ref[...], lnb_ref[...])
    x2_ref[...] = x2
    x2b_ref[...] = x2.astype(BF16)


def _combine(slot, ys, x1e, lng, lnb):
    t = x1e.shape[0]
    tm = min(COMBINE_TM, t)
    d = D_MODEL
    return pl.pallas_call(
        _combine_kernel,
        grid_spec=pltpu.PrefetchScalarGridSpec(
            num_scalar_prefetch=1, grid=(t // tm,),
            in_specs=[pl.BlockSpec(memory_space=pl.ANY),
                      pl.BlockSpec((tm, d), lambda i, s: (i, 0)),
                      pl.BlockSpec((1, d), lambda i, s: (0, 0)), pl.BlockSpec((1, d), lambda i, s: (0, 0))],
            out_specs=[pl.BlockSpec((tm, d), lambda i, s: (i, 0)), pl.BlockSpec((tm, d), lambda i, s: (i, 0))],
            scratch_shapes=[pltpu.VMEM((2, tm, d), F32), pltpu.SemaphoreType.DMA((2,))]),
        out_shape=[jax.ShapeDtypeStruct((t, d), F32), jax.ShapeDtypeStruct((t, d), BF16)],
        compiler_params=_cparams(("arbitrary",)),
        name="moe_combine_ln",
    )(slot, ys, x1e, lng, lnb)


def _routing_tables(ridx, cnt, t):
    tm = MOE_TM
    n_tiles = t // tm + N_SEG
    counts = cnt[0, :N_SEG].astype(jnp.int32)
    padded = ((counts + tm - 1) // tm) * tm
    ends = jnp.cumsum(padded)
    offs = ends - padded
    slot = offs[ridx[:, 0]] + ridx[:, 1]
    n_used = ends[-1] // tm
    tile_blk = jnp.minimum(jnp.arange(n_tiles, dtype=jnp.int32), n_used - 1)
    tile_seg = jnp.minimum(jnp.sum(tile_blk[:, None] * tm >= ends[None, :], axis=1), N_SEG - 1)
    tile_valid = jnp.clip(offs[tile_seg] + counts[tile_seg] - tile_blk * tm, 0, tm)
    first = (tile_seg // N_PAIRS) * EXPERTS_PER_GROUP
    tile_ea = first + jnp.asarray(PAIR_A, jnp.int32)[tile_seg % N_PAIRS]
    tile_eb = first + jnp.asarray(PAIR_B, jnp.int32)[tile_seg % N_PAIRS]
    i32 = lambda a: a.astype(jnp.int32)
    return (i32(slot), i32(offs + counts), i32(ends), i32(tile_blk), i32(tile_ea), i32(tile_eb), i32(tile_valid),
            i32(n_used.reshape(1)), n_tiles * tm)


def _lanes(vec, start):
    return jnp.zeros((1, LANES), F32).at[0, start:start + vec.shape[0]].set(vec.astype(F32))


def _layer_params(w_in, conv_a_w, a_log_a, dt_bias_a, norm_a, w_lr_b, b_lr_b, norm_b, conv_c_w, conv_c_b,
                  a_log_c, dt_bias_c, d_c, norm_c, w_oa, w_ob, w_oc, w_out, ln1_g, ln1_b, w_rg, w_re,
                  w_gate_e, w_up_e, w_down_e, ln2_g, ln2_b):
    def cols(a, n):
        return w_in[:, a:a + n]

    w_big = jnp.concatenate([
        cols(_GATES, 3 * D_MODEL), cols(_QA, 1024), cols(_KA, 1024), cols(_VA, 1024), cols(_ZA, 1024),
        cols(_QB, 512), cols(_KB, 512), cols(_VB, 1024), cols(_RB, 1024),
        cols(_ZC, 1024), cols(_XBC, 1536)], axis=1).astype(BF16)
    w_small = jnp.concatenate([cols(_BETA, 8), cols(_ALPHA_A, 8), cols(_LRB, 16), cols(_DTC, 16),
                               jnp.zeros((D_MODEL, LANES - 48), F32)], axis=1).astype(BF16)
    w_r = jnp.concatenate([w_rg, w_re, jnp.zeros((D_MODEL, LANES - N_GROUPS - N_EXPERTS), F32)], axis=1)
    w_rh = w_r.astype(BF16)
    w_rl = (w_r - w_rh.astype(F32)).astype(BF16)
    wlr = jnp.zeros((LANES, B_HEADS * B_DK), F32).at[S_LRB:S_LRB + B_GATE_RANK].set(w_lr_b).astype(BF16)
    row = lambda v: v.reshape(1, -1).astype(F32)
    return dict(
        w_big=w_big, w_small=w_small,
        cwq=conv_a_w[:, :1024], cwk=conv_a_w[:, 1024:2048], cwv=conv_a_w[:, 2048:],
        alog_a=_lanes(a_log_a, S_ALPHA), dtb_a=_lanes(dt_bias_a, S_ALPHA), gain_a=row(norm_a),
        wlr=wlr, blr=row(b_lr_b), gain_b=row(norm_b),
        cwx=conv_c_w[:, :C_DINNER], cbx=row(conv_c_b[:C_DINNER]),
        cwb=conv_c_w[:, C_DINNER:C_DINNER + 256], cbb=row(conv_c_b[C_DINNER:C_DINNER + 256]),
        cwc=conv_c_w[:, C_DINNER + 256:], cbc=row(conv_c_b[C_DINNER + 256:]),
        alog_c=_lanes(a_log_c, S_DT), dtb_c=_lanes(dt_bias_c, S_DT),
        dskip=row(jnp.repeat(d_c, C_HEADDIM)), gain_c=row(norm_c),
        woa=w_oa.astype(BF16), wob=w_ob.astype(BF16), woc=w_oc.astype(BF16), wout=w_out.astype(BF16),
        w_rh=w_rh, w_rl=w_rl, ln1_g=row(ln1_g), ln1_b=row(ln1_b),
        wg=w_gate_e.astype(BF16), wu=w_up_e.astype(BF16), wd=w_down_e.astype(BF16),
        ln2_g=row(ln2_g), ln2_b=row(ln2_b))


def _layer(x, xb, pr, bsz, length):
    t = bsz * length
    p, psm = _inproj(xb, pr["w_big"], pr["w_small"])
    oa = _mixer_a(p, psm, pr["cwq"], pr["cwk"], pr["cwv"], pr["alog_a"], pr["dtb_a"], pr["gain_a"], bsz, length)
    ob = _mixer_b(p, psm, pr["wlr"], pr["blr"], pr["gain_b"], bsz, length)
    oc = _mixer_c(p, psm, pr["cwx"], pr["cbx"], pr["cwb"], pr["cbb"], pr["cwc"], pr["cbc"],
                  pr["alog_c"], pr["dtb_c"], pr["dskip"], pr["gain_c"], bsz, length)
    x1e, ridx, cnt = _post(oa, ob, oc, p, x, pr["woa"], pr["wob"], pr["woc"], pr["wout"],
                           pr["w_rh"], pr["w_rl"], pr["ln1_g"], pr["ln1_b"])
    slot, pad_lo, pad_hi, tile_blk, tile_ea, tile_eb, tile_valid, n_used, n_slots = _routing_tables(ridx, cnt, t)
    xs = _dispatch(slot, pad_lo, pad_hi, x1e, n_slots)
    ys = _experts(tile_blk, tile_ea, tile_eb, tile_valid, n_used, xs, pr["wg"], pr["wu"], pr["wd"])
    return _combine(slot, ys, x1e, pr["ln2_g"], pr["ln2_b"])


def kernel(x, w_in, conv_a_w, a_log_a, dt_bias_a, norm_a, w_lr_b, b_lr_b, norm_b, conv_c_w, conv_c_b, a_log_c,
           dt_bias_c, d_c, norm_c, w_oa, w_ob, w_oc, w_out, ln1_g, ln1_b, w_rg, w_re, w_gate_e, w_up_e,
           w_down_e, ln2_g, ln2_b):
    bsz, length, d = x.shape
    assert d == D_MODEL and length % SUPER == 0
    params = (w_in, conv_a_w, a_log_a, dt_bias_a, norm_a, w_lr_b, b_lr_b, norm_b, conv_c_w, conv_c_b, a_log_c,
              dt_bias_c, d_c, norm_c, w_oa, w_ob, w_oc, w_out, ln1_g, ln1_b, w_rg, w_re, w_gate_e, w_up_e,
              w_down_e, ln2_g, ln2_b)
    xf = x.reshape(bsz * length, d)
    xb = xf.astype(BF16)
    for layer in range(params[0].shape[0]):
        pr = _layer_params(*(p[layer] for p in params))
        xf, xb = _layer(xf, xb, pr, bsz, length)
    return xf.reshape(bsz, length, d)
```

```python
import functools

import jax
import jax.numpy as jnp
import numpy as np
from jax import lax
from jax.experimental import pallas as pl
from jax.experimental.pallas import tpu as pltpu

F32 = jnp.float32
BF16 = jnp.bfloat16

D_MODEL = 2048
DEPTH = 2
CHUNK = 64
CHUNK_SHIFT = 6
CONV_W = 4
A_HEADS, A_DK, A_DV = 8, 128, 128
B_HEADS, B_DK, B_DV = 4, 128, 256
B_GATE_RANK = 16
B_GATE_NORM = 16.0
C_HEADS, C_HEADDIM, C_GROUPS, C_DSTATE = 16, 64, 2, 128
C_DINNER = C_HEADS * C_HEADDIM
C_GROUP_W = C_DINNER // C_GROUPS
C_HEADS_PER_GROUP = C_HEADS // C_GROUPS
N_GROUPS, EXPERTS_PER_GROUP = 4, 4
N_EXPERTS = N_GROUPS * EXPERTS_PER_GROUP
D_EXPERT = 768
PAIR_A = (0, 0, 0, 1, 1, 3)
PAIR_B = (1, 2, 3, 3, 2, 2)
N_PAIRS = len(PAIR_A)
N_SEG = N_GROUPS * N_PAIRS
LN_EPS = 1e-5
RMS_EPS = 1e-6
ALPHA = (2 * DEPTH) ** 0.25

_QA, _KA, _VA, _ZA, _BETA, _ALPHA_A = 0, 1024, 2048, 3072, 4096, 4104
_QB, _KB, _VB, _RB, _LRB = 4112, 4624, 5136, 6160, 7184
_ZC, _XBC, _DTC, _GATES = 7200, 8224, 9760, 9776
P_GATES, P_QA, P_KA, P_VA, P_ZA = 0, 6144, 7168, 8192, 9216
P_QB, P_KB, P_VB, P_RB = 10240, 10752, 11264, 12288
P_ZC, P_XC, P_BC, P_CC = 13312, 14336, 15360, 15616
P_WIDTH = 15872
S_BETA, S_ALPHA, S_LRB, S_DT = 0, 8, 16, 32
LANES = 128
XE_W = D_MODEL + LANES
ZERO_ROWS = 64

VMEM_LIMIT_BYTES = 56 * 1024 * 1024

SUPER = 256
A_HEADS_PER_STEP = 8
B_HEADS_PER_STEP = 4
PROJ_TM, PROJ_TN = 2048, 512
POST_TM = 512
ROUTE_ROWS = 128
MOE_TM = 512
DISPATCH_TM = 1024
COMBINE_TM = 512


def _cparams(sem):
    return pltpu.CompilerParams(dimension_semantics=sem, vmem_limit_bytes=VMEM_LIMIT_BYTES)


def _dot(a, b):
    return jnp.dot(a, b, preferred_element_type=F32)


def _dot_nt(a, b):
    return lax.dot_general(a, b, (((1,), (1,)), ((), ())), preferred_element_type=F32)


def _dot_tn(a, b):
    return lax.dot_general(a, b, (((0,), (0,)), ((), ())), preferred_element_type=F32)


def _split3(x):
    hi = x.astype(BF16)
    r = x - hi.astype(F32)
    mid = r.astype(BF16)
    lo = (r - mid.astype(F32)).astype(BF16)
    return hi, mid, lo


def _dot_sel(sel_bf16, x):
    hi, mid, lo = _split3(x)
    return _dot(sel_bf16, hi) + _dot(sel_bf16, mid) + _dot(sel_bf16, lo)


def _x_dot_sel(x, sel_bf16):
    hi, mid, lo = _split3(x)
    return _dot(hi, sel_bf16) + _dot(mid, sel_bf16) + _dot(lo, sel_bf16)


def _silu(x):
    return x * jax.nn.sigmoid(x)


def _softplus(x):
    return jnp.maximum(x, 0.0) + jnp.log1p(jnp.exp(-jnp.abs(x)))


def _lane_col(arr, lane_iota, idx):
    return jnp.sum(jnp.where(lane_iota == idx, arr, 0.0), axis=1, keepdims=True)


def _chunk_masks(n):
    ri = lax.broadcasted_iota(jnp.int32, (n, n), 0)
    ci = lax.broadcasted_iota(jnp.int32, (n, n), 1)
    same = (ri >> CHUNK_SHIFT) == (ci >> CHUNK_SHIFT)
    return ri, ci, same, same & (ci <= ri)


def _chunk_cumsum(causal, x):
    n = x.shape[0]
    cum = _dot_sel(jnp.where(causal, 1.0, 0.0).astype(BF16), x)
    total = jnp.concatenate(
        [jnp.broadcast_to(cum[c * CHUNK + CHUNK - 1:(c + 1) * CHUNK, :], (CHUNK, x.shape[1]))
         for c in range(n // CHUNK)], axis=0)
    return cum, total


def _shift_selector(n):
    ri = lax.broadcasted_iota(jnp.int32, ((CONV_W - 1) * n, n), 0)
    ci = lax.broadcasted_iota(jnp.int32, ((CONV_W - 1) * n, n), 1)
    s = ri // n + 1
    return jnp.where(ci == ri - (s - 1) * n - s, 1.0, 0.0).astype(BF16)


def _conv_silu(xb, w_ref, carry_ref, shift_sel, bias=None):
    n, width = xb.shape
    x = xb.astype(F32)
    shifted = _dot(shift_sel, xb)
    acc = w_ref[CONV_W - 1:CONV_W, :] * x
    prev = jnp.concatenate([carry_ref[...], jnp.zeros((8, width), F32)], axis=0)
    head = jnp.zeros((8, width), F32)
    for s in range(1, CONV_W):
        w_s = w_ref[CONV_W - 1 - s:CONV_W - s, :]
        acc = acc + w_s * shifted[(s - 1) * n:s * n]
        head = head + w_s * prev[8 - s:16 - s]
    carry_ref[...] = x[n - 8:n, :]
    acc = jnp.concatenate([acc[0:8] + head, acc[8:]], axis=0)
    if bias is not None:
        acc = acc + bias
    return _silu(acc)


def _inproj_kernel(x_ref, wb_ref, ws_ref, ob_ref, os_ref):
    x = x_ref[...]
    ob_ref[...] = _dot(x, wb_ref[...]).astype(ob_ref.dtype)

    @pl.when(pl.program_id(1) == 0)
    def _():
        os_ref[...] = _dot(x, ws_ref[...])


def _inproj(xb, w_big, w_small):
    t = xb.shape[0]
    tm = min(PROJ_TM, t)
    assert t % tm == 0 and P_WIDTH % PROJ_TN == 0
    return pl.pallas_call(
        _inproj_kernel,
        grid=(t // tm, P_WIDTH // PROJ_TN),
        in_specs=[pl.BlockSpec((tm, D_MODEL), lambda i, j: (i, 0)),
                  pl.BlockSpec((D_MODEL, PROJ_TN), lambda i, j: (0, j)),
                  pl.BlockSpec((D_MODEL, LANES), lambda i, j: (0, 0))],
        out_specs=[pl.BlockSpec((tm, PROJ_TN), lambda i, j: (i, j)),
                   pl.BlockSpec((tm, LANES), lambda i, j: (i, 0))],
        out_shape=[jax.ShapeDtypeStruct((t, P_WIDTH), BF16), jax.ShapeDtypeStruct((t, LANES), F32)],
        compiler_params=_cparams(("parallel", "arbitrary")),
        name="inproj",
    )(xb, w_big, w_small)


def _mixa_kernel(q_ref, k_ref, v_ref, z_ref, sm_ref, cwq_ref, cwk_ref, cwv_ref, alog_ref, dtb_ref, gain_ref,
                 o_ref, s_ref, cq_ref, ck_ref, cv_ref, gct_ref, *, hb):
    hg = pl.program_id(1)
    n = q_ref.shape[0]

    @pl.when(pl.program_id(2) == 0)
    def _():
        s_ref[...] = jnp.zeros_like(s_ref)
        cq_ref[...] = jnp.zeros_like(cq_ref)
        ck_ref[...] = jnp.zeros_like(ck_ref)
        cv_ref[...] = jnp.zeros_like(cv_ref)

    shift_sel = _shift_selector(n)
    q = _conv_silu(q_ref[...], cwq_ref, cq_ref, shift_sel)
    k = _conv_silu(k_ref[...], cwk_ref, ck_ref, shift_sel)
    v = _conv_silu(v_ref[...], cwv_ref, cv_ref, shift_sel)

    sm = sm_ref[...]
    lane = lax.broadcasted_iota(jnp.int32, (n, LANES), 1)
    beta_all = jax.nn.sigmoid(sm)
    g_all = -jnp.exp(alog_ref[...]) * _softplus(sm + dtb_ref[...])
    ri, ci, same, causal = _chunk_masks(n)
    strict = same & (ci < ri)
    gc, gl = _chunk_cumsum(causal, g_all)
    egc, ekd, edl = jnp.exp(gc), jnp.exp(gl - gc), jnp.exp(gl)
    gct_ref[...] = gc.T
    lev = 31 - lax.clz(ri ^ ci)
    eye = jnp.where(ri == ci, 1.0, 0.0)

    heads = range(hb)
    sls = [slice(j * A_DK, (j + 1) * A_DK) for j in heads]
    e_dls, a_s, attns, rhs, wqs, kds = [], [], [], [], [], []
    for j in heads:
        h = hg * hb + j
        qh, kh, vh = q[:, sls[j]], k[:, sls[j]], v[:, sls[j]]
        qh = qh * lax.rsqrt(jnp.sum(qh * qh, axis=1, keepdims=True) + 1e-6) * (A_DK ** -0.5)
        kh = kh * lax.rsqrt(jnp.sum(kh * kh, axis=1, keepdims=True) + 1e-6)
        beta = _lane_col(beta_all, lane, S_BETA + h)
        gcol = _lane_col(gc, lane, S_ALPHA + h)
        e_gc = _lane_col(egc, lane, S_ALPHA + h)
        e_kd = _lane_col(ekd, lane, S_ALPHA + h)
        e_dls.append(_lane_col(edl, lane, S_ALPHA + h))
        grow = gct_ref[pl.ds(S_ALPHA + h, 1), :]
        dec = jnp.exp(jnp.where(causal, gcol - grow, -jnp.inf))
        kb = kh * beta
        kkqk = _dot_nt(jnp.concatenate([kb, qh], axis=0).astype(BF16), kh.astype(BF16))
        a_s.append(jnp.where(strict, kkqk[:n] * dec, 0.0))
        attns.append((kkqk[n:] * dec).astype(BF16))
        rhs.append(jnp.concatenate([vh * beta, kb * e_gc], axis=1).astype(BF16))
        wqs.append((qh * e_gc).astype(BF16))
        kds.append((kh * e_kd).astype(BF16))
    xs = [eye - jnp.where(lev == 0, a, 0.0) for a in a_s]
    for ls in range(1, CHUNK_SHIFT):
        level = lev == ls
        xbs = [x.astype(BF16) for x in xs]
        ts = [_dot(xb, jnp.where(level, a, 0.0).astype(BF16)).astype(BF16) for xb, a in zip(xbs, a_s)]
        xs = [x - _dot(t, xb) for x, t, xb in zip(xs, ts, xbs)]
    uws = [_dot(x.astype(BF16), r) for x, r in zip(xs, rhs)]
    ws = [uw[:, A_DV:].astype(BF16) for uw in uws]
    ss = [s_ref[j] for j in heads]
    v_new = [[] for _ in heads]
    o_inter = [[] for _ in heads]
    for c in range(n // CHUNK):
        r = slice(c * CHUNK, (c + 1) * CHUNK)
        res = [_dot(jnp.concatenate([ws[j][r], wqs[j][r]], axis=0), ss[j].astype(BF16)) for j in heads]
        for j in heads:
            o_inter[j].append(res[j][CHUNK:])
            v_new[j].append((uws[j][r, :A_DV] - res[j][:CHUNK]).astype(BF16))
        ss = [ss[j] * e_dls[j][c * CHUNK:c * CHUNK + 1, :] + _dot_tn(kds[j][r], v_new[j][c]) for j in heads]
    for j in heads:
        s_ref[j] = ss[j]
        o = jnp.concatenate(o_inter[j], axis=0) + _dot(attns[j], jnp.concatenate(v_new[j], axis=0))
        o = o * lax.rsqrt(jnp.mean(o * o, axis=1, keepdims=True) + RMS_EPS) * gain_ref[...]
        o_ref[:, sls[j]] = (o * _silu(z_ref[:, sls[j]].astype(F32))).astype(o_ref.dtype)


def _mixer_a(p, psm, cwq, cwk, cwv, alog, dtb, gain, bsz, length):
    hb = A_HEADS_PER_STEP
    w = hb * A_DK
    nl = length // SUPER

    def col(base):
        return lambda b, h, l: (b * nl + l, base // w + h)

    vec = lambda b, h, l: (0, 0)
    cw = lambda b, h, l: (0, h)
    return pl.pallas_call(
        functools.partial(_mixa_kernel, hb=hb),
        grid=(bsz, A_HEADS // hb, nl),
        in_specs=[pl.BlockSpec((SUPER, w), col(P_QA)), pl.BlockSpec((SUPER, w), col(P_KA)),
                  pl.BlockSpec((SUPER, w), col(P_VA)), pl.BlockSpec((SUPER, w), col(P_ZA)),
                  pl.BlockSpec((SUPER, LANES), lambda b, h, l: (b * nl + l, 0)),
                  pl.BlockSpec((CONV_W, w), cw), pl.BlockSpec((CONV_W, w), cw), pl.BlockSpec((CONV_W, w), cw),
                  pl.BlockSpec((1, LANES), vec), pl.BlockSpec((1, LANES), vec), pl.BlockSpec((1, A_DV), vec)],
        out_specs=pl.BlockSpec((SUPER, w), lambda b, h, l: (b * nl + l, h)),
        out_shape=jax.ShapeDtypeStruct((bsz * length, A_HEADS * A_DV), BF16),
        scratch_shapes=[pltpu.VMEM((hb, A_DK, A_DV), F32),
                        pltpu.VMEM((8, w), F32), pltpu.VMEM((8, w), F32), pltpu.VMEM((8, w), F32),
                        pltpu.VMEM((LANES, SUPER), F32)],
        compiler_params=_cparams(("parallel", "parallel", "arbitrary")),
        name="mixer_a",
    )(p, p, p, p, psm, cwq, cwk, cwv, alog, dtb, gain)


def _mixb_kernel(q_ref, k_ref, v_ref, r_ref, sm_ref, wlr_ref, blr_ref, gain_ref, o_ref, st_ref, *, hb):
    n = q_ref.shape[0]

    @pl.when(pl.program_id(2) == 0)
    def _():
        st_ref[...] = jnp.zeros_like(st_ref)

    zl = _dot(sm_ref[...].astype(BF16), wlr_ref[...]) + blr_ref[...]
    log_a = -_softplus(-zl) * (1.0 / B_GATE_NORM)
    _, _, same, causal = _chunk_masks(n)
    bc, bl = _chunk_cumsum(causal, log_a)
    q = q_ref[...].astype(F32) * (B_DK ** -0.5)
    k = k_ref[...].astype(F32)
    qe = (q * jnp.exp(bc)).astype(BF16)
    ke = (k * jnp.exp(-bc)).astype(BF16)
    kd = (k * jnp.exp(bl - bc)).astype(BF16)
    edl = jnp.exp(bl)

    heads = range(hb)
    chunks = range(n // CHUNK)
    rows = [slice(c * CHUNK, (c + 1) * CHUNK) for c in chunks]
    sls = [slice(j * B_DK, (j + 1) * B_DK) for j in heads]
    svs = [slice(j * B_DV, (j + 1) * B_DV) for j in heads]
    vhs = [v_ref[:, svs[j]] for j in heads]
    upd = [[_dot_tn(vhs[j][r], kd[r, sls[j]]) for r in rows] for j in heads]
    attn = [jnp.where(causal, _dot_nt(qe[:, sls[j]], ke[:, sls[j]]), 0.0).astype(BF16) for j in heads]
    intra = [_dot(attn[j], vhs[j]) for j in heads]
    for j in heads:
        st = st_ref[j]
        inter = []
        for c in chunks:
            inter.append(_dot_nt(qe[rows[c], sls[j]], st.astype(BF16)))
            st = st * edl[c * CHUNK:c * CHUNK + 1, sls[j]] + upd[j][c]
        st_ref[j] = st
        o = intra[j] + jnp.concatenate(inter, axis=0)
        o = o * lax.rsqrt(jnp.mean(o * o, axis=1, keepdims=True) + RMS_EPS) * gain_ref[...]
        o_ref[:, svs[j]] = (o * _silu(r_ref[:, svs[j]].astype(F32))).astype(o_ref.dtype)


def _mixer_b(p, psm, wlr, blr, gain, bsz, length):
    hb = B_HEADS_PER_STEP
    wk, wv = hb * B_DK, hb * B_DV
    nl = length // SUPER

    def col(base, width):
        return lambda b, h, l: (b * nl + l, base // width + h)

    return pl.pallas_call(
        functools.partial(_mixb_kernel, hb=hb),
        grid=(bsz, B_HEADS // hb, nl),
        in_specs=[pl.BlockSpec((SUPER, wk), col(P_QB, wk)), pl.BlockSpec((SUPER, wk), col(P_KB, wk)),
                  pl.BlockSpec((SUPER, wv), col(P_VB, wv)), pl.BlockSpec((SUPER, wv), col(P_RB, wv)),
                  pl.BlockSpec((SUPER, LANES), lambda b, h, l: (b * nl + l, 0)),
                  pl.BlockSpec((LANES, wk), lambda b, h, l: (0, h)),
                  pl.BlockSpec((1, wk), lambda b, h, l: (0, h)),
                  pl.BlockSpec((1, B_DV), lambda b, h, l: (0, 0))],
        out_specs=pl.BlockSpec((SUPER, wv), lambda b, h, l: (b * nl + l, h)),
        out_shape=jax.ShapeDtypeStruct((bsz * length, B_HEADS * B_DV), BF16),
        scratch_shapes=[pltpu.VMEM((hb, B_DV, B_DK), F32)],
        compiler_params=_cparams(("parallel", "parallel", "arbitrary")),
        name="mixer_b",
    )(p, p, p, p, psm, wlr, blr, gain)


def _mixc_kernel(z_ref, x_ref, b_ref, c_ref, sm_ref, cwx_ref, cbx_ref, cwb_ref, cbb_ref, cwc_ref, cbc_ref,
                 alog_ref, dtb_ref, dskip_ref, gain_ref, o_ref,
                 st_ref, cx_ref, cbc_ref_, cst_ref):
    g = pl.program_id(1)
    n = x_ref.shape[0]

    @pl.when(pl.program_id(2) == 0)
    def _():
        st_ref[...] = jnp.zeros_like(st_ref)
        cx_ref[...] = jnp.zeros_like(cx_ref)
        cbc_ref_[...] = jnp.zeros_like(cbc_ref_)

    shift_sel = _shift_selector(n)
    xs = _conv_silu(x_ref[...], cwx_ref, cx_ref, shift_sel, cbx_ref[...])
    bcm = _conv_silu(jnp.concatenate([b_ref[...], c_ref[...]], axis=1),
                     jnp.concatenate([cwb_ref[...], cwc_ref[...]], axis=1), cbc_ref_, shift_sel,
                     jnp.concatenate([cbb_ref[...], cbc_ref[...]], axis=1)).astype(BF16)
    bm, cm = bcm[:, :C_DSTATE], bcm[:, C_DSTATE:]

    lane = lax.broadcasted_iota(jnp.int32, (n, LANES), 1)
    dt_all = _softplus(sm_ref[...] + dtb_ref[...])
    da = dt_all * (-jnp.exp(alog_ref[...]))
    _, _, same, causal = _chunk_masks(n)
    cs, _ = _chunk_cumsum(causal, da)
    cst_ref[...] = cs.T
    er = lax.broadcasted_iota(jnp.int32, (LANES, C_GROUP_W), 0)
    ec = lax.broadcasted_iota(jnp.int32, (LANES, C_GROUP_W), 1)
    head0 = S_DT + g * C_HEADS_PER_GROUP
    expand = jnp.where(er == head0 + ec // C_HEADDIM, 1.0, 0.0).astype(BF16)
    dt_e = _dot(dt_all.astype(BF16), expand)
    cs_e = _x_dot_sel(cs, expand)
    csl_e = jnp.concatenate(
        [jnp.broadcast_to(cs_e[c * CHUNK + CHUNK - 1:(c + 1) * CHUNK, :], (CHUNK, C_GROUP_W))
         for c in range(n // CHUNK)], axis=0)
    ecs, edl = jnp.exp(cs_e), jnp.exp(csl_e)
    xdt = xs * dt_e
    xdt_b = xdt.astype(BF16)
    xdtw = (xdt * jnp.exp(csl_e - cs_e)).astype(BF16)
    scores = jnp.where(causal, _dot_nt(cm, bm), 0.0)

    half = lax.broadcasted_iota(jnp.int32, (n, LANES), 1) // C_HEADDIM
    y_intra = []
    for pair in range(C_HEADS_PER_GROUP // 2):
        xp = xdt_b[:, pair * LANES:(pair + 1) * LANES]
        acc = None
        for sub in range(2):
            hidx = head0 + 2 * pair + sub
            ccol = _lane_col(cs, lane, hidx)
            crow = cst_ref[pl.ds(hidx, 1), :]
            dec = jnp.exp(jnp.where(causal, ccol - crow, -jnp.inf))
            part = _dot((scores * dec).astype(BF16), jnp.where(half == sub, xp, jnp.zeros_like(xp)))
            acc = part if acc is None else acc + part
        y_intra.append(acc)
    y = jnp.concatenate(y_intra, axis=1)

    st = st_ref[...]
    rows = [slice(c * CHUNK, (c + 1) * CHUNK) for c in range(n // CHUNK)]
    upd = [_dot_tn(bm[r], xdtw[r]) for r in rows]
    y_inter = []
    for c, r in enumerate(rows):
        y_inter.append(_dot(cm[r], st.astype(BF16)) * ecs[r])
        st = st * edl[c * CHUNK:c * CHUNK + 1, :] + upd[c]
    st_ref[...] = st
    y = y + jnp.concatenate(y_inter, axis=0) + xs * dskip_ref[...]
    yz = y * _silu(z_ref[...].astype(F32))
    yz = yz * lax.rsqrt(jnp.mean(yz * yz, axis=1, keepdims=True) + RMS_EPS) * gain_ref[...]
    o_ref[...] = yz.astype(o_ref.dtype)


def _mixer_c(p, psm, cwx, cbx, cwb, cbb, cwc, cbc, alog, dtb, dskip, gain, bsz, length):
    nl = length // SUPER
    gw, ds = C_GROUP_W, C_DSTATE

    def col(base, width):
        return lambda b, g, l: (b * nl + l, base // width + g)

    grp = lambda b, g, l: (0, g)
    vec = lambda b, g, l: (0, 0)
    return pl.pallas_call(
        _mixc_kernel,
        grid=(bsz, C_GROUPS, nl),
        in_specs=[pl.BlockSpec((SUPER, gw), col(P_ZC, gw)), pl.BlockSpec((SUPER, gw), col(P_XC, gw)),
                  pl.BlockSpec((SUPER, ds), col(P_BC, ds)), pl.BlockSpec((SUPER, ds), col(P_CC, ds)),
                  pl.BlockSpec((SUPER, LANES), lambda b, g, l: (b * nl + l, 0)),
                  pl.BlockSpec((CONV_W, gw), grp), pl.BlockSpec((1, gw), grp),
                  pl.BlockSpec((CONV_W, ds), grp), pl.BlockSpec((1, ds), grp),
                  pl.BlockSpec((CONV_W, ds), grp), pl.BlockSpec((1, ds), grp),
                  pl.BlockSpec((1, LANES), vec), pl.BlockSpec((1, LANES), vec),
                  pl.BlockSpec((1, gw), grp), pl.BlockSpec((1, gw), grp)],
        out_specs=pl.BlockSpec((SUPER, gw), lambda b, g, l: (b * nl + l, g)),
        out_shape=jax.ShapeDtypeStruct((bsz * length, C_DINNER), BF16),
        scratch_shapes=[pltpu.VMEM((ds, gw), F32),
                        pltpu.VMEM((8, gw), F32), pltpu.VMEM((8, 2 * ds), F32),
                        pltpu.VMEM((LANES, SUPER), F32)],
        compiler_params=_cparams(("parallel", "parallel", "arbitrary")),
        name="mixer_c",
    )(p, p, p, p, psm, cwx, cbx, cwb, cbb, cwc, cbc, alog, dtb, dskip, gain)


def _layernorm(y, g, b):
    mu = jnp.mean(y, axis=1, keepdims=True)
    yc = y - mu
    var = jnp.mean(yc * yc, axis=1, keepdims=True)
    return yc * lax.rsqrt(var + LN_EPS) * g + b


def _merge_kernel(oa_ref, ob_ref, oc_ref, ga_ref, gb_ref, gc_ref, woa_ref, wob_ref, woc_ref, m_ref):
    merged = jax.nn.sigmoid(ga_ref[...].astype(F32)) * _dot(oa_ref[...], woa_ref[...])
    merged = merged + jax.nn.sigmoid(gb_ref[...].astype(F32)) * _dot(ob_ref[...], wob_ref[...])
    merged = merged + jax.nn.sigmoid(gc_ref[...].astype(F32)) * _dot(oc_ref[...], woc_ref[...])
    m_ref[...] = merged.astype(m_ref.dtype)


def _merge(oa, ob, oc, p, woa, wob, woc):
    t = oa.shape[0]
    tm = min(POST_TM, t)
    d = D_MODEL
    row = lambda i: (i, 0)
    wspec = lambda shape: pl.BlockSpec(shape, lambda i: (0, 0), pipeline_mode=pl.Buffered(1))
    return pl.pallas_call(
        _merge_kernel,
        grid=(t // tm,),
        in_specs=[pl.BlockSpec((tm, 1024), row), pl.BlockSpec((tm, 1024), row), pl.BlockSpec((tm, 1024), row),
                  pl.BlockSpec((tm, d), lambda i: (i, P_GATES // d)),
                  pl.BlockSpec((tm, d), lambda i: (i, P_GATES // d + 1)),
                  pl.BlockSpec((tm, d), lambda i: (i, P_GATES // d + 2)),
                  wspec((1024, d)), wspec((1024, d)), wspec((1024, d))],
        out_specs=pl.BlockSpec((tm, d), row),
        out_shape=jax.ShapeDtypeStruct((t, d), BF16),
        compiler_params=_cparams(("parallel",)),
        name="merge",
    )(oa, ob, oc, p, p, p, woa, wob, woc)


def _route(logits, carry):
    tm = logits.shape[0]
    lane = lax.broadcasted_iota(jnp.int32, (tm, LANES), 1)
    neg = -jnp.inf
    is_g = lane < N_GROUPS
    gl = jnp.where(is_g, logits, neg)
    gmax = jnp.max(gl, axis=1, keepdims=True)
    gsum = jnp.sum(jnp.where(is_g, jnp.exp(logits - gmax), 0.0), axis=1, keepdims=True)
    p_g = 1.0 / gsum
    g_sel = jnp.min(jnp.where(gl == gmax, lane, LANES), axis=1, keepdims=True)
    in_g = (lane >= N_GROUPS) & (lane < N_GROUPS + N_EXPERTS) & (((lane - N_GROUPS) >> 2) == g_sel)
    emax = jnp.max(jnp.where(in_g, logits, neg), axis=1, keepdims=True)
    ee = jnp.where(in_g, jnp.exp(logits - emax), 0.0)
    probs = jnp.where(in_g, ee / jnp.sum(ee, axis=1, keepdims=True), -1.0)
    p1 = jnp.max(probs, axis=1, keepdims=True)
    i1 = jnp.min(jnp.where(probs == p1, lane, LANES), axis=1, keepdims=True)
    probs2 = jnp.where(lane == i1, -1.0, probs)
    p2 = jnp.max(probs2, axis=1, keepdims=True)
    i2 = jnp.min(jnp.where(probs2 == p2, lane, LANES), axis=1, keepdims=True)
    psum = p1 + p2
    w1 = p_g * (p1 / psum)
    w2 = p_g * (p2 / psum)
    l1 = (i1 - N_GROUPS) & (EXPERTS_PER_GROUP - 1)
    l2 = (i2 - N_GROUPS) & (EXPERTS_PER_GROUP - 1)
    la = jnp.minimum(l1, l2)
    lb = jnp.maximum(l1, l2)
    pair = jnp.where(la == 0, lb - 1, jnp.where(la == 1, 6 - lb, 5))
    seg = g_sel * N_PAIRS + pair
    first_is_a = l1 == jnp.where(pair < 3, PAIR_A[0], jnp.where(pair < 5, PAIR_A[3], PAIR_A[5]))
    wa = jnp.where(first_is_a, w1, w2)
    wb = jnp.where(first_is_a, w2, w1)

    oh = lane == seg
    cnt = jnp.where(oh, 1.0, 0.0)
    ri = lax.broadcasted_iota(jnp.int32, (tm, tm), 0)
    ci = lax.broadcasted_iota(jnp.int32, (tm, tm), 1)
    before = _dot(jnp.where(ci < ri, 1.0, 0.0).astype(BF16), cnt.astype(BF16)) + carry
    rank = jnp.sum(jnp.where(oh, before, 0.0), axis=1, keepdims=True).astype(jnp.int32)
    total = carry + jnp.sum(cnt, axis=0, keepdims=True)
    packed = jnp.where(lane == 0, seg, jnp.where(lane == 1, rank, jnp.zeros_like(lane)))
    return packed.T[0:8, :], jnp.where(lane == 0, wa, jnp.where(lane == 1, wb, 0.0)), total


def _post_kernel(m_ref, x_ref, wout_ref, wr_ref, lng_ref, lnb_ref, x1e_ref, ridx_ref, cnt_ref, carry_ref):
    tm = x_ref.shape[0]

    @pl.when(pl.program_id(0) == 0)
    def _():
        carry_ref[...] = jnp.zeros_like(carry_ref)

    y = ALPHA * x_ref[...] + _dot(m_ref[...], wout_ref[...])
    x1 = _layernorm(y, lng_ref[...], lnb_ref[...])
    x1e_ref[:, :D_MODEL] = x1

    xh = x1.astype(BF16)
    xl = (x1 - xh.astype(F32)).astype(BF16)
    prod = _dot(xh, wr_ref[...]) + _dot(xl, wr_ref[...])
    logits = prod[:, :LANES] + prod[:, LANES:]
    sub = min(ROUTE_ROWS, tm)
    total = carry_ref[0:1, :]
    for r0 in range(0, tm, sub):
        rows_t, w_lanes, total = _route(logits[r0:r0 + sub], total)
        ridx_ref[:, r0:r0 + sub] = rows_t
        x1e_ref[r0:r0 + sub, D_MODEL:] = w_lanes
    carry_ref[...] = jnp.broadcast_to(total, carry_ref.shape)
    cnt_ref[...] = jnp.broadcast_to(total, cnt_ref.shape)


def _post(merged, x, wout, wr, lng, lnb):
    t = x.shape[0]
    tm = min(POST_TM, t)
    d = D_MODEL
    row = lambda i: (i, 0)
    const = lambda i: (0, 0)
    wspec = lambda shape: pl.BlockSpec(shape, const, pipeline_mode=pl.Buffered(1))
    return pl.pallas_call(
        _post_kernel,
        grid=(t // tm,),
        in_specs=[pl.BlockSpec((tm, d), row), pl.BlockSpec((tm, d), row),
                  wspec((d, d)), wspec((d, 2 * LANES)), wspec((1, d)), wspec((1, d))],
        out_specs=[pl.BlockSpec((tm, XE_W), row), pl.BlockSpec((8, tm), lambda i: (0, i)),
                   pl.BlockSpec((8, LANES), const)],
        out_shape=[jax.ShapeDtypeStruct((t, XE_W), F32), jax.ShapeDtypeStruct((8, t), jnp.int32),
                   jax.ShapeDtypeStruct((8, LANES), F32)],
        scratch_shapes=[pltpu.VMEM((8, LANES), F32)],
        compiler_params=_cparams(("arbitrary",)),
        name="ln_router",
    )(merged, x, wout, wr, lng, lnb)


def _row_copy(src_ref, src_row, dst_ref, dst_row, sem):
    return pltpu.make_async_copy(src_ref.at[pl.ds(src_row, 1), :], dst_ref.at[pl.ds(dst_row, 1), :], sem)


def _dispatch_kernel(slot_ref, pad_lo_ref, pad_hi_ref, x_ref, xs_ref, zbuf_ref, sem):
    tm = x_ref.shape[0]
    step = pl.program_id(0)
    base = step * tm

    @pl.when(step == 0)
    def _():
        zbuf_ref[...] = jnp.zeros_like(zbuf_ref)

        def zero_copy(lo, kk):
            return _row_copy(zbuf_ref, 0, xs_ref, lo + kk, sem)

        def bucket(sg, carry):
            lo = pad_lo_ref[sg]
            n_copies = pad_hi_ref[sg] - lo

            def start(kk, c):
                zero_copy(lo, kk).start()
                return c

            def wait(kk, c):
                zero_copy(lo, kk).wait()
                return c

            lax.fori_loop(0, n_copies, start, 0)
            lax.fori_loop(0, n_copies, wait, 0)
            return carry

        lax.fori_loop(0, N_SEG, bucket, 0)

        zr = zbuf_ref.shape[0]
        tail_lo = pad_hi_ref[N_SEG - 1]

        def tail_copy(kk):
            return pltpu.make_async_copy(zbuf_ref, xs_ref.at[pl.ds(pl.multiple_of(tail_lo + kk * zr, zr), zr), :], sem)

        def tail_start(kk, c):
            tail_copy(kk).start()
            return c

        def tail_wait(kk, c):
            tail_copy(kk).wait()
            return c

        n_tail = (xs_ref.shape[0] - tail_lo) // zr
        lax.fori_loop(0, n_tail, tail_start, 0)
        lax.fori_loop(0, n_tail, tail_wait, 0)

    def start_row(r, carry):
        _row_copy(x_ref, r, xs_ref, slot_ref[base + r], sem).start()
        return carry

    lax.fori_loop(0, tm, start_row, 0, unroll=8)
    pltpu.make_async_copy(x_ref, xs_ref.at[pl.ds(0, tm), :], sem).wait()


def _dispatch(slot, pad_lo, pad_hi, x1e, n_slots):
    t = x1e.shape[0]
    tm = min(DISPATCH_TM, t)
    return pl.pallas_call(
        _dispatch_kernel,
        grid_spec=pltpu.PrefetchScalarGridSpec(
            num_scalar_prefetch=3, grid=(t // tm,),
            in_specs=[pl.BlockSpec((tm, XE_W), lambda i, s, lo, hi: (i, 0))],
            out_specs=pl.BlockSpec(memory_space=pl.ANY),
            scratch_shapes=[pltpu.VMEM((ZERO_ROWS, XE_W), F32), pltpu.SemaphoreType.DMA(())]),
        out_shape=jax.ShapeDtypeStruct((n_slots, XE_W), F32),
        compiler_params=_cparams(("arbitrary",)),
        name="moe_dispatch",
    )(slot, pad_lo, pad_hi, x1e)


def _expert_kernel(blk_ref, ea_ref, eb_ref, valid_ref, nused_ref, x_ref, wga_ref, wua_ref, wda_ref,
                   wgb_ref, wub_ref, wdb_ref, y_ref):
    del blk_ref, ea_ref, eb_ref
    tm = x_ref.shape[0]
    half = tm // 2
    step = pl.program_id(0)
    used = step < nused_ref[0]
    full = valid_ref[step] > half

    def run(rows):
        x = x_ref[0:rows, :]
        xb = x[:, :D_MODEL].astype(BF16)
        ha = (_silu(_dot(xb, wga_ref[0])) * _dot(xb, wua_ref[0])).astype(BF16)
        hb = (_silu(_dot(xb, wgb_ref[0])) * _dot(xb, wub_ref[0])).astype(BF16)
        wa = x[:, D_MODEL:D_MODEL + 1]
        wb = x[:, D_MODEL + 1:D_MODEL + 2]
        y_ref[0:rows, :] = wa * _dot(ha, wda_ref[0]) + wb * _dot(hb, wdb_ref[0])

    @pl.when(used & full)
    def _():
        run(tm)

    @pl.when(used & jnp.logical_not(full))
    def _():
        run(half)
        y_ref[half:tm, :] = jnp.zeros((tm - half, D_MODEL), F32)

    @pl.when(jnp.logical_not(used))
    def _():
        y_ref[...] = jnp.zeros_like(y_ref)


def _experts(tile_blk, tile_ea, tile_eb, tile_valid, n_used, xs, wg, wu, wd):
    tm = MOE_TM
    n_tiles = tile_blk.shape[0]
    d, de = D_MODEL, D_EXPERT
    one = pl.Buffered(1)

    def wspec(shape, which):
        return pl.BlockSpec(shape, lambda i, tb, ea, eb, nv, nu: ((ea, eb)[which][i], 0, 0), pipeline_mode=one)

    return pl.pallas_call(
        _expert_kernel,
        grid_spec=pltpu.PrefetchScalarGridSpec(
            num_scalar_prefetch=5, grid=(n_tiles,),
            in_specs=[pl.BlockSpec((tm, XE_W), lambda i, tb, ea, eb, nv, nu: (tb[i], 0)),
                      wspec((1, d, de), 0), wspec((1, d, de), 0), wspec((1, de, d), 0),
                      wspec((1, d, de), 1), wspec((1, d, de), 1), wspec((1, de, d), 1)],
            out_specs=pl.BlockSpec((tm, d), lambda i, tb, ea, eb, nv, nu: (i, 0))),
        out_shape=jax.ShapeDtypeStruct((n_tiles * tm, d), F32),
        compiler_params=_cparams(("arbitrary",)),
        name="moe_experts",
    )(tile_blk, tile_ea, tile_eb, tile_valid, n_used, xs, wg, wu, wd, wg, wu, wd)


def _combine_kernel(slot_ref, ys_ref, x1_ref, lng_ref, lnb_ref, x2_ref, x2b_ref, ybuf_ref, sem):
    tm = x1_ref.shape[0]
    step = pl.program_id(0)

    def gather(s):
        buf = s % 2

        def start_row(r, carry):
            _row_copy(ys_ref, slot_ref[s * tm + r], ybuf_ref.at[buf], r, sem.at[buf]).start()
            return carry

        lax.fori_loop(0, tm, start_row, 0, unroll=8)

    @pl.when(step == 0)
    def _():
        gather(step)

    @pl.when(step + 1 < pl.num_programs(0))
    def _():
        gather(step + 1)

    buf = step % 2
    pltpu.make_async_copy(ys_ref.at[pl.ds(0, tm), :], ybuf_ref.at[buf], sem.at[buf]).wait()
    x2 = _layernorm(ALPHA * x1_ref[...] + ybuf_ref[buf], lng_ref[...], lnb_ref[...])
    x2_ref[...] = x2
    x2b_ref[...] = x2.astype(BF16)


def _combine(slot, ys, x1e, lng, lnb):
    t = x1e.shape[0]
    tm = min(COMBINE_TM, t)
    d = D_MODEL
    return pl.pallas_call(
        _combine_kernel,
        grid_spec=pltpu.PrefetchScalarGridSpec(
            num_scalar_prefetch=1, grid=(t // tm,),
            in_specs=[pl.BlockSpec(memory_space=pl.ANY),
                      pl.BlockSpec((tm, d), lambda i, s: (i, 0)),
                      pl.BlockSpec((1, d), lambda i, s: (0, 0)), pl.BlockSpec((1, d), lambda i, s: (0, 0))],
            out_specs=[pl.BlockSpec((tm, d), lambda i, s: (i, 0)), pl.BlockSpec((tm, d), lambda i, s: (i, 0))],
            scratch_shapes=[pltpu.VMEM((2, tm, d), F32), pltpu.SemaphoreType.DMA((2,))]),
        out_shape=[jax.ShapeDtypeStruct((t, d), F32), jax.ShapeDtypeStruct((t, d), BF16)],
        compiler_params=_cparams(("arbitrary",)),
        name="moe_combine_ln",
    )(slot, ys, x1e, lng, lnb)


def _routing_tables(ridx, cnt, t):
    tm = MOE_TM
    n_tiles = t // tm + N_SEG
    counts = cnt[0, :N_SEG].astype(jnp.int32)
    padded = ((counts + tm - 1) // tm) * tm
    ends = jnp.cumsum(padded)
    offs = ends - padded
    slot = offs[ridx[0]] + ridx[1]
    n_used = ends[-1] // tm
    tile_blk = jnp.minimum(jnp.arange(n_tiles, dtype=jnp.int32), n_used - 1)
    tile_seg = jnp.minimum(jnp.sum(tile_blk[:, None] * tm >= ends[None, :], axis=1), N_SEG - 1)
    tile_valid = jnp.clip(offs[tile_seg] + counts[tile_seg] - tile_blk * tm, 0, tm)
    first = (tile_seg // N_PAIRS) * EXPERTS_PER_GROUP
    tile_ea = first + jnp.asarray(PAIR_A, jnp.int32)[tile_seg % N_PAIRS]
    tile_eb = first + jnp.asarray(PAIR_B, jnp.int32)[tile_seg % N_PAIRS]
    i32 = lambda a: a.astype(jnp.int32)
    return (i32(slot), i32(offs + counts), i32(ends), i32(tile_blk), i32(tile_ea), i32(tile_eb), i32(tile_valid),
            i32(n_used.reshape(1)), n_tiles * tm)


def _lanes(vec, start):
    return jnp.zeros((1, LANES), F32).at[0, start:start + vec.shape[0]].set(vec.astype(F32))


def _layer_params(w_in, conv_a_w, a_log_a, dt_bias_a, norm_a, w_lr_b, b_lr_b, norm_b, conv_c_w, conv_c_b,
                  a_log_c, dt_bias_c, d_c, norm_c, w_oa, w_ob, w_oc, w_out, ln1_g, ln1_b, w_rg, w_re,
                  w_gate_e, w_up_e, w_down_e, ln2_g, ln2_b):
    def cols(a, n):
        return w_in[:, a:a + n]

    w_big = jnp.concatenate([
        cols(_GATES, 3 * D_MODEL), cols(_QA, 1024), cols(_KA, 1024), cols(_VA, 1024), cols(_ZA, 1024),
        cols(_QB, 512), cols(_KB, 512), cols(_VB, 1024), cols(_RB, 1024),
        cols(_ZC, 1024), cols(_XBC, 1536)], axis=1).astype(BF16)
    w_small = jnp.concatenate([cols(_BETA, 8), cols(_ALPHA_A, 8), cols(_LRB, 16), cols(_DTC, 16),
                               jnp.zeros((D_MODEL, LANES - 48), F32)], axis=1).astype(BF16)
    w_r = jnp.concatenate([w_rg, w_re, jnp.zeros((D_MODEL, LANES - N_GROUPS - N_EXPERTS), F32)], axis=1)
    w_rh = w_r.astype(BF16)
    w_rl = (w_r - w_rh.astype(F32)).astype(BF16)
    wlr = jnp.zeros((LANES, B_HEADS * B_DK), F32).at[S_LRB:S_LRB + B_GATE_RANK].set(w_lr_b).astype(BF16)
    row = lambda v: v.reshape(1, -1).astype(F32)
    return dict(
        w_big=w_big, w_small=w_small,
        cwq=conv_a_w[:, :1024], cwk=conv_a_w[:, 1024:2048], cwv=conv_a_w[:, 2048:],
        alog_a=_lanes(a_log_a, S_ALPHA), dtb_a=_lanes(dt_bias_a, S_ALPHA), gain_a=row(norm_a),
        wlr=wlr, blr=row(b_lr_b), gain_b=row(norm_b),
        cwx=conv_c_w[:, :C_DINNER], cbx=row(conv_c_b[:C_DINNER]),
        cwb=conv_c_w[:, C_DINNER:C_DINNER + 256], cbb=row(conv_c_b[C_DINNER:C_DINNER + 256]),
        cwc=conv_c_w[:, C_DINNER + 256:], cbc=row(conv_c_b[C_DINNER + 256:]),
        alog_c=_lanes(a_log_c, S_DT), dtb_c=_lanes(dt_bias_c, S_DT),
        dskip=row(jnp.repeat(d_c, C_HEADDIM)), gain_c=row(norm_c),
        woa=w_oa.astype(BF16), wob=w_ob.astype(BF16), woc=w_oc.astype(BF16), wout=w_out.astype(BF16),
        w_r=jnp.concatenate([w_rh, w_rl], axis=1), ln1_g=row(ln1_g), ln1_b=row(ln1_b),
        wg=w_gate_e.astype(BF16), wu=w_up_e.astype(BF16), wd=w_down_e.astype(BF16),
        ln2_g=row(ln2_g), ln2_b=row(ln2_b))


def _layer(x, xb, pr, bsz, length):
    t = bsz * length
    p, psm = _inproj(xb, pr["w_big"], pr["w_small"])
    oa = _mixer_a(p, psm, pr["cwq"], pr["cwk"], pr["cwv"], pr["alog_a"], pr["dtb_a"], pr["gain_a"], bsz, length)
    ob = _mixer_b(p, psm, pr["wlr"], pr["blr"], pr["gain_b"], bsz, length)
    oc = _mixer_c(p, psm, pr["cwx"], pr["cbx"], pr["cwb"], pr["cbb"], pr["cwc"], pr["cbc"],
                  pr["alog_c"], pr["dtb_c"], pr["dskip"], pr["gain_c"], bsz, length)
    merged = _merge(oa, ob, oc, p, pr["woa"], pr["wob"], pr["woc"])
    x1e, ridx, cnt = _post(merged, x, pr["wout"], pr["w_r"], pr["ln1_g"], pr["ln1_b"])
    slot, pad_lo, pad_hi, tile_blk, tile_ea, tile_eb, tile_valid, n_used, n_slots = _routing_tables(ridx, cnt, t)
    xs = _dispatch(slot, pad_lo, pad_hi, x1e, n_slots)
    ys = _experts(tile_blk, tile_ea, tile_eb, tile_valid, n_used, xs, pr["wg"], pr["wu"], pr["wd"])
    return _combine(slot, ys, x1e, pr["ln2_g"], pr["ln2_b"])


def kernel(x, w_in, conv_a_w, a_log_a, dt_bias_a, norm_a, w_lr_b, b_lr_b, norm_b, conv_c_w, conv_c_b, a_log_c,
           dt_bias_c, d_c, norm_c, w_oa, w_ob, w_oc, w_out, ln1_g, ln1_b, w_rg, w_re, w_gate_e, w_up_e,
           w_down_e, ln2_g, ln2_b):
    bsz, length, d = x.shape
    assert d == D_MODEL and length % SUPER == 0
    params = (w_in, conv_a_w, a_log_a, dt_bias_a, norm_a, w_lr_b, b_lr_b, norm_b, conv_c_w, conv_c_b, a_log_c,
              dt_bias_c, d_c, norm_c, w_oa, w_ob, w_oc, w_out, ln1_g, ln1_b, w_rg, w_re, w_gate_e, w_up_e,
              w_down_e, ln2_g, ln2_b)
    xf = x.reshape(bsz * length, d)
    xb = xf.astype(BF16)
    for layer in range(params[0].shape[0]):
        pr = _layer_params(*(p[layer] for p in params))
        xf, xb = _layer(xf, xb, pr, bsz, length)
    return xf.reshape(bsz, length, d)
```

```python
import functools

import jax
import jax.numpy as jnp
import numpy as np
from jax import lax
from jax.experimental import pallas as pl
from jax.experimental.pallas import tpu as pltpu

F32 = jnp.float32
BF16 = jnp.bfloat16

D_MODEL = 2048
DEPTH = 2
CHUNK = 64
CHUNK_SHIFT = 6
CONV_W = 4
A_HEADS, A_DK, A_DV = 8, 128, 128
B_HEADS, B_DK, B_DV = 4, 128, 256
B_GATE_RANK = 16
B_GATE_NORM = 16.0
C_HEADS, C_HEADDIM, C_GROUPS, C_DSTATE = 16, 64, 2, 128
C_DINNER = C_HEADS * C_HEADDIM
C_GROUP_W = C_DINNER // C_GROUPS
C_HEADS_PER_GROUP = C_HEADS // C_GROUPS
N_GROUPS, EXPERTS_PER_GROUP = 4, 4
N_EXPERTS = N_GROUPS * EXPERTS_PER_GROUP
D_EXPERT = 768
PAIR_A = (0, 0, 0, 1, 1, 3)
PAIR_B = (1, 2, 3, 3, 2, 2)
N_PAIRS = len(PAIR_A)
N_SEG = N_GROUPS * N_PAIRS
LN_EPS = 1e-5
RMS_EPS = 1e-6
ALPHA = (2 * DEPTH) ** 0.25

_QA, _KA, _VA, _ZA, _BETA, _ALPHA_A = 0, 1024, 2048, 3072, 4096, 4104
_QB, _KB, _VB, _RB, _LRB = 4112, 4624, 5136, 6160, 7184
_ZC, _XBC, _DTC, _GATES = 7200, 8224, 9760, 9776
P_GATES, P_QA, P_KA, P_VA, P_ZA = 0, 6144, 7168, 8192, 9216
P_QB, P_KB, P_VB, P_RB = 10240, 10752, 11264, 12288
P_ZC, P_XC, P_BC, P_CC = 13312, 14336, 15360, 15616
P_WIDTH = 15872
S_BETA, S_ALPHA, S_LRB, S_DT = 0, 8, 16, 32
LANES = 128
XE_W = D_MODEL + LANES
ZERO_ROWS = 64

VMEM_LIMIT_BYTES = 56 * 1024 * 1024

SUPER = 256
SUB = 128
A_HEADS_PER_STEP = 8
B_HEADS_PER_STEP = 4
PROJ_TM, PROJ_TN = 2048, 512
POST_TM = 512
ROUTE_ROWS = 128
MOE_TM = 512
DISPATCH_TM = 1024
COMBINE_TM = 512


def _cparams(sem):
    return pltpu.CompilerParams(dimension_semantics=sem, vmem_limit_bytes=VMEM_LIMIT_BYTES)


def _dot(a, b):
    return jnp.dot(a, b, preferred_element_type=F32)


def _dot_nt(a, b):
    return lax.dot_general(a, b, (((1,), (1,)), ((), ())), preferred_element_type=F32)


def _dot_tn(a, b):
    return lax.dot_general(a, b, (((0,), (0,)), ((), ())), preferred_element_type=F32)


def _split3(x):
    hi = x.astype(BF16)
    r = x - hi.astype(F32)
    mid = r.astype(BF16)
    lo = (r - mid.astype(F32)).astype(BF16)
    return hi, mid, lo


def _dot_sel(sel_bf16, x):
    hi, mid, lo = _split3(x)
    return _dot(sel_bf16, hi) + _dot(sel_bf16, mid) + _dot(sel_bf16, lo)


def _x_dot_sel(x, sel_bf16):
    hi, mid, lo = _split3(x)
    return _dot(hi, sel_bf16) + _dot(mid, sel_bf16) + _dot(lo, sel_bf16)


def _silu(x):
    return x * jax.nn.sigmoid(x)


def _softplus(x):
    return jnp.maximum(x, 0.0) + jnp.log1p(jnp.exp(-jnp.abs(x)))


def _lane_col(arr, lane_iota, idx):
    return jnp.sum(jnp.where(lane_iota == idx, arr, 0.0), axis=1, keepdims=True)


def _chunk_masks(n):
    ri = lax.broadcasted_iota(jnp.int32, (n, n), 0)
    ci = lax.broadcasted_iota(jnp.int32, (n, n), 1)
    same = (ri >> CHUNK_SHIFT) == (ci >> CHUNK_SHIFT)
    return ri, ci, same, same & (ci <= ri)


def _chunk_cumsum(causal, x):
    n = x.shape[0]
    cum = _dot_sel(jnp.where(causal, 1.0, 0.0).astype(BF16), x)
    total = jnp.concatenate(
        [jnp.broadcast_to(cum[c * CHUNK + CHUNK - 1:(c + 1) * CHUNK, :], (CHUNK, x.shape[1]))
         for c in range(n // CHUNK)], axis=0)
    return cum, total


def _shift_selector(n):
    ri = lax.broadcasted_iota(jnp.int32, ((CONV_W - 1) * n, n), 0)
    ci = lax.broadcasted_iota(jnp.int32, ((CONV_W - 1) * n, n), 1)
    s = ri // n + 1
    return jnp.where(ci == ri - (s - 1) * n - s, 1.0, 0.0).astype(BF16)


def _conv_silu(xb, w_ref, carry_ref, shift_sel, bias=None):
    n, width = xb.shape
    x = xb.astype(F32)
    shifted = _dot(shift_sel, xb)
    acc = w_ref[CONV_W - 1:CONV_W, :] * x
    prev = jnp.concatenate([carry_ref[...], jnp.zeros((8, width), F32)], axis=0)
    head = jnp.zeros((8, width), F32)
    for s in range(1, CONV_W):
        w_s = w_ref[CONV_W - 1 - s:CONV_W - s, :]
        acc = acc + w_s * shifted[(s - 1) * n:s * n]
        head = head + w_s * prev[8 - s:16 - s]
    carry_ref[...] = x[n - 8:n, :]
    acc = jnp.concatenate([acc[0:8] + head, acc[8:]], axis=0)
    if bias is not None:
        acc = acc + bias
    return _silu(acc)


def _inproj_kernel(x_ref, wb_ref, ws_ref, ob_ref, os_ref):
    x = x_ref[...]
    ob_ref[...] = _dot(x, wb_ref[...]).astype(ob_ref.dtype)

    @pl.when(pl.program_id(1) == 0)
    def _():
        os_ref[...] = _dot(x, ws_ref[...])


def _inproj(xb, w_big, w_small):
    t = xb.shape[0]
    tm = min(PROJ_TM, t)
    assert t % tm == 0 and P_WIDTH % PROJ_TN == 0
    return pl.pallas_call(
        _inproj_kernel,
        grid=(t // tm, P_WIDTH // PROJ_TN),
        in_specs=[pl.BlockSpec((tm, D_MODEL), lambda i, j: (i, 0)),
                  pl.BlockSpec((D_MODEL, PROJ_TN), lambda i, j: (0, j)),
                  pl.BlockSpec((D_MODEL, LANES), lambda i, j: (0, 0))],
        out_specs=[pl.BlockSpec((tm, PROJ_TN), lambda i, j: (i, j)),
                   pl.BlockSpec((tm, LANES), lambda i, j: (i, 0))],
        out_shape=[jax.ShapeDtypeStruct((t, P_WIDTH), BF16), jax.ShapeDtypeStruct((t, LANES), F32)],
        compiler_params=_cparams(("parallel", "arbitrary")),
        name="inproj",
    )(xb, w_big, w_small)


def _mixa_kernel(q_ref, k_ref, v_ref, z_ref, sm_ref, cwq_ref, cwk_ref, cwv_ref, alog_ref, dtb_ref, gain_ref,
                 o_ref, s_ref, cq_ref, ck_ref, cv_ref, gct_ref, *, hb):
    hg = pl.program_id(1)
    n = q_ref.shape[0]

    @pl.when(pl.program_id(2) == 0)
    def _():
        s_ref[...] = jnp.zeros_like(s_ref)
        cq_ref[...] = jnp.zeros_like(cq_ref)
        ck_ref[...] = jnp.zeros_like(ck_ref)
        cv_ref[...] = jnp.zeros_like(cv_ref)

    shift_sel = _shift_selector(n)
    q = _conv_silu(q_ref[...], cwq_ref, cq_ref, shift_sel)
    k = _conv_silu(k_ref[...], cwk_ref, ck_ref, shift_sel)
    v = _conv_silu(v_ref[...], cwv_ref, cv_ref, shift_sel)

    sm = sm_ref[...]
    lane = lax.broadcasted_iota(jnp.int32, (n, LANES), 1)
    beta_all = jax.nn.sigmoid(sm)
    g_all = -jnp.exp(alog_ref[...]) * _softplus(sm + dtb_ref[...])
    _, _, _, causal = _chunk_masks(n)
    gc, gl = _chunk_cumsum(causal, g_all)
    egc, ekd, edl = jnp.exp(gc), jnp.exp(gl - gc), jnp.exp(gl)
    gct_ref[...] = gc.T

    sub = SUB
    nb = n // sub
    assert nb == 2
    ri = lax.broadcasted_iota(jnp.int32, (sub, n), 0)
    ci = lax.broadcasted_iota(jnp.int32, (sub, n), 1) & (sub - 1)
    same = (ri >> CHUNK_SHIFT) == (ci >> CHUNK_SHIFT)
    causal_c = same & (ci <= ri)
    strict_c = same & (ci < ri)
    lev = 31 - lax.clz(ri ^ ci)
    eye = jnp.where(ri == ci, 1.0, 0.0)
    zblk = jnp.zeros((sub, sub), BF16)

    def _blockdiag(m):
        return jnp.concatenate([jnp.concatenate([m[:, :sub], zblk], axis=1),
                                jnp.concatenate([zblk, m[:, sub:]], axis=1)], axis=0)

    def _side_by_side(m):
        return jnp.concatenate([m[:sub], m[sub:]], axis=1)

    heads = range(hb)
    sls = [slice(j * A_DK, (j + 1) * A_DK) for j in heads]
    e_dls, a_s, attns, rhs, wqs, kds = [], [], [], [], [], []
    for j in heads:
        h = hg * hb + j
        qh, kh, vh = q[:, sls[j]], k[:, sls[j]], v[:, sls[j]]
        qh = qh * lax.rsqrt(jnp.sum(qh * qh, axis=1, keepdims=True) + 1e-6) * (A_DK ** -0.5)
        kh = kh * lax.rsqrt(jnp.sum(kh * kh, axis=1, keepdims=True) + 1e-6)
        beta = _lane_col(beta_all, lane, S_BETA + h)
        gcol = _lane_col(gc, lane, S_ALPHA + h)
        e_gc = _lane_col(egc, lane, S_ALPHA + h)
        e_kd = _lane_col(ekd, lane, S_ALPHA + h)
        e_dls.append(_lane_col(edl, lane, S_ALPHA + h))
        grow = gct_ref[pl.ds(S_ALPHA + h, 1), :]
        gcol_c = jnp.concatenate([jnp.broadcast_to(gcol[:sub], (sub, sub)),
                                  jnp.broadcast_to(gcol[sub:], (sub, sub))], axis=1)
        dec = jnp.exp(jnp.where(causal_c, gcol_c - grow, -jnp.inf))
        kb = kh * beta
        khb = kh.astype(BF16)
        lhs = jnp.concatenate([_side_by_side(kb), _side_by_side(qh)], axis=0).astype(BF16)
        kkqk = _dot_nt(lhs, _blockdiag(_side_by_side(khb)))
        a_s.append(jnp.where(strict_c, kkqk[:sub] * dec, 0.0))
        attns.append((kkqk[sub:] * dec).astype(BF16))
        rhs.append(jnp.concatenate([vh * beta, kb * e_gc], axis=1).astype(BF16))
        wqs.append((qh * e_gc).astype(BF16))
        kds.append((kh * e_kd).astype(BF16))
    xs = [eye - jnp.where(lev == 0, a, 0.0) for a in a_s]
    for ls in range(1, CHUNK_SHIFT):
        level = lev == ls
        xbs = [x.astype(BF16) for x in xs]
        ts = [_dot(xb, _blockdiag(jnp.where(level, a, 0.0).astype(BF16))).astype(BF16) for xb, a in zip(xbs, a_s)]
        xs = [x - _dot(t, _blockdiag(xb)) for x, t, xb in zip(xs, ts, xbs)]
    uws = []
    for x, r in zip(xs, rhs):
        xb = x.astype(BF16)
        uws.append(jnp.concatenate([_dot(xb[:, b * sub:(b + 1) * sub], r[b * sub:(b + 1) * sub])
                                    for b in range(nb)], axis=0))
    ws = [uw[:, A_DV:].astype(BF16) for uw in uws]
    ss = [s_ref[j] for j in heads]
    v_new = [[] for _ in heads]
    o_inter = [[] for _ in heads]
    for c in range(n // CHUNK):
        r = slice(c * CHUNK, (c + 1) * CHUNK)
        res = [_dot(jnp.concatenate([ws[j][r], wqs[j][r]], axis=0), ss[j].astype(BF16)) for j in heads]
        for j in heads:
            o_inter[j].append(res[j][CHUNK:])
            v_new[j].append((uws[j][r, :A_DV] - res[j][:CHUNK]).astype(BF16))
        ss = [ss[j] * e_dls[j][c * CHUNK:c * CHUNK + 1, :] + _dot_tn(kds[j][r], v_new[j][c]) for j in heads]
    for j in heads:
        s_ref[j] = ss[j]
        o_intra = _dot(attns[j], _blockdiag(_side_by_side(jnp.concatenate(v_new[j], axis=0))))
        o = jnp.concatenate(o_inter[j], axis=0) + jnp.concatenate([o_intra[:, :A_DV], o_intra[:, A_DV:]], axis=0)
        o = o * lax.rsqrt(jnp.mean(o * o, axis=1, keepdims=True) + RMS_EPS) * gain_ref[...]
        o_ref[:, sls[j]] = (o * _silu(z_ref[:, sls[j]].astype(F32))).astype(o_ref.dtype)


def _mixer_a(p, psm, cwq, cwk, cwv, alog, dtb, gain, bsz, length):
    hb = A_HEADS_PER_STEP
    w = hb * A_DK
    nl = length // SUPER

    def col(base):
        return lambda b, h, l: (b * nl + l, base // w + h)

    vec = lambda b, h, l: (0, 0)
    cw = lambda b, h, l: (0, h)
    return pl.pallas_call(
        functools.partial(_mixa_kernel, hb=hb),
        grid=(bsz, A_HEADS // hb, nl),
        in_specs=[pl.BlockSpec((SUPER, w), col(P_QA)), pl.BlockSpec((SUPER, w), col(P_KA)),
                  pl.BlockSpec((SUPER, w), col(P_VA)), pl.BlockSpec((SUPER, w), col(P_ZA)),
                  pl.BlockSpec((SUPER, LANES), lambda b, h, l: (b * nl + l, 0)),
                  pl.BlockSpec((CONV_W, w), cw), pl.BlockSpec((CONV_W, w), cw), pl.BlockSpec((CONV_W, w), cw),
                  pl.BlockSpec((1, LANES), vec), pl.BlockSpec((1, LANES), vec), pl.BlockSpec((1, A_DV), vec)],
        out_specs=pl.BlockSpec((SUPER, w), lambda b, h, l: (b * nl + l, h)),
        out_shape=jax.ShapeDtypeStruct((bsz * length, A_HEADS * A_DV), BF16),
        scratch_shapes=[pltpu.VMEM((hb, A_DK, A_DV), F32),
                        pltpu.VMEM((8, w), F32), pltpu.VMEM((8, w), F32), pltpu.VMEM((8, w), F32),
                        pltpu.VMEM((LANES, SUPER), F32)],
        compiler_params=_cparams(("parallel", "parallel", "arbitrary")),
        name="mixer_a",
    )(p, p, p, p, psm, cwq, cwk, cwv, alog, dtb, gain)


def _mixb_kernel(q_ref, k_ref, v_ref, r_ref, sm_ref, wlr_ref, blr_ref, gain_ref, o_ref, st_ref, *, hb):
    n = q_ref.shape[0]

    @pl.when(pl.program_id(2) == 0)
    def _():
        st_ref[...] = jnp.zeros_like(st_ref)

    zl = _dot(sm_ref[...].astype(BF16), wlr_ref[...]) + blr_ref[...]
    log_a = -_softplus(-zl) * (1.0 / B_GATE_NORM)
    _, _, same, causal = _chunk_masks(n)
    bc, bl = _chunk_cumsum(causal, log_a)
    q = q_ref[...].astype(F32) * (B_DK ** -0.5)
    k = k_ref[...].astype(F32)
    qe = (q * jnp.exp(bc)).astype(BF16)
    ke = (k * jnp.exp(-bc)).astype(BF16)
    kd = (k * jnp.exp(bl - bc)).astype(BF16)
    edl = jnp.exp(bl)

    heads = range(hb)
    chunks = range(n // CHUNK)
    rows = [slice(c * CHUNK, (c + 1) * CHUNK) for c in chunks]
    sls = [slice(j * B_DK, (j + 1) * B_DK) for j in heads]
    svs = [slice(j * B_DV, (j + 1) * B_DV) for j in heads]
    vhs = [v_ref[:, svs[j]] for j in heads]
    upd = [[_dot_tn(vhs[j][r], kd[r, sls[j]]) for r in rows] for j in heads]
    attn = [jnp.where(causal, _dot_nt(qe[:, sls[j]], ke[:, sls[j]]), 0.0).astype(BF16) for j in heads]
    intra = [_dot(attn[j], vhs[j]) for j in heads]
    for j in heads:
        st = st_ref[j]
        inter = []
        for c in chunks:
            inter.append(_dot_nt(qe[rows[c], sls[j]], st.astype(BF16)))
            st = st * edl[c * CHUNK:c * CHUNK + 1, sls[j]] + upd[j][c]
        st_ref[j] = st
        o = intra[j] + jnp.concatenate(inter, axis=0)
        o = o * lax.rsqrt(jnp.mean(o * o, axis=1, keepdims=True) + RMS_EPS) * gain_ref[...]
        o_ref[:, svs[j]] = (o * _silu(r_ref[:, svs[j]].astype(F32))).astype(o_ref.dtype)


def _mixer_b(p, psm, wlr, blr, gain, bsz, length):
    hb = B_HEADS_PER_STEP
    wk, wv = hb * B_DK, hb * B_DV
    nl = length // SUPER

    def col(base, width):
        return lambda b, h, l: (b * nl + l, base // width + h)

    return pl.pallas_call(
        functools.partial(_mixb_kernel, hb=hb),
        grid=(bsz, B_HEADS // hb, nl),
        in_specs=[pl.BlockSpec((SUPER, wk), col(P_QB, wk)), pl.BlockSpec((SUPER, wk), col(P_KB, wk)),
                  pl.BlockSpec((SUPER, wv), col(P_VB, wv)), pl.BlockSpec((SUPER, wv), col(P_RB, wv)),
                  pl.BlockSpec((SUPER, LANES), lambda b, h, l: (b * nl + l, 0)),
                  pl.BlockSpec((LANES, wk), lambda b, h, l: (0, h)),
                  pl.BlockSpec((1, wk), lambda b, h, l: (0, h)),
                  pl.BlockSpec((1, B_DV), lambda b, h, l: (0, 0))],
        out_specs=pl.BlockSpec((SUPER, wv), lambda b, h, l: (b * nl + l, h)),
        out_shape=jax.ShapeDtypeStruct((bsz * length, B_HEADS * B_DV), BF16),
        scratch_shapes=[pltpu.VMEM((hb, B_DV, B_DK), F32)],
        compiler_params=_cparams(("parallel", "parallel", "arbitrary")),
        name="mixer_b",
    )(p, p, p, p, psm, wlr, blr, gain)


def _mixc_kernel(z_ref, x_ref, b_ref, c_ref, sm_ref, cwx_ref, cbx_ref, cwb_ref, cbb_ref, cwc_ref, cbc_ref,
                 alog_ref, dtb_ref, dskip_ref, gain_ref, o_ref,
                 st_ref, cx_ref, cbc_ref_, cst_ref):
    g = pl.program_id(1)
    n = x_ref.shape[0]

    @pl.when(pl.program_id(2) == 0)
    def _():
        st_ref[...] = jnp.zeros_like(st_ref)
        cx_ref[...] = jnp.zeros_like(cx_ref)
        cbc_ref_[...] = jnp.zeros_like(cbc_ref_)

    shift_sel = _shift_selector(n)
    xs = _conv_silu(x_ref[...], cwx_ref, cx_ref, shift_sel, cbx_ref[...])
    bcm = _conv_silu(jnp.concatenate([b_ref[...], c_ref[...]], axis=1),
                     jnp.concatenate([cwb_ref[...], cwc_ref[...]], axis=1), cbc_ref_, shift_sel,
                     jnp.concatenate([cbb_ref[...], cbc_ref[...]], axis=1)).astype(BF16)
    bm, cm = bcm[:, :C_DSTATE], bcm[:, C_DSTATE:]

    lane = lax.broadcasted_iota(jnp.int32, (n, LANES), 1)
    dt_all = _softplus(sm_ref[...] + dtb_ref[...])
    da = dt_all * (-jnp.exp(alog_ref[...]))
    _, _, same, causal = _chunk_masks(n)
    cs, _ = _chunk_cumsum(causal, da)
    cst_ref[...] = cs.T
    er = lax.broadcasted_iota(jnp.int32, (LANES, C_GROUP_W), 0)
    ec = lax.broadcasted_iota(jnp.int32, (LANES, C_GROUP_W), 1)
    head0 = S_DT + g * C_HEADS_PER_GROUP
    expand = jnp.where(er == head0 + ec // C_HEADDIM, 1.0, 0.0).astype(BF16)
    dt_e = _dot(dt_all.astype(BF16), expand)
    cs_e = _x_dot_sel(cs, expand)
    csl_e = jnp.concatenate(
        [jnp.broadcast_to(cs_e[c * CHUNK + CHUNK - 1:(c + 1) * CHUNK, :], (CHUNK, C_GROUP_W))
         for c in range(n // CHUNK)], axis=0)
    ecs, edl = jnp.exp(cs_e), jnp.exp(csl_e)
    xdt = xs * dt_e
    xdt_b = xdt.astype(BF16)
    xdtw = (xdt * jnp.exp(csl_e - cs_e)).astype(BF16)
    scores = jnp.where(causal, _dot_nt(cm, bm), 0.0)

    half = lax.broadcasted_iota(jnp.int32, (n, LANES), 1) // C_HEADDIM
    y_intra = []
    for pair in range(C_HEADS_PER_GROUP // 2):
        xp = xdt_b[:, pair * LANES:(pair + 1) * LANES]
        acc = None
        for sub in range(2):
            hidx = head0 + 2 * pair + sub
            ccol = _lane_col(cs, lane, hidx)
            crow = cst_ref[pl.ds(hidx, 1), :]
            dec = jnp.exp(jnp.where(causal, ccol - crow, -jnp.inf))
            part = _dot((scores * dec).astype(BF16), jnp.where(half == sub, xp, jnp.zeros_like(xp)))
            acc = part if acc is None else acc + part
        y_intra.append(acc)
    y = jnp.concatenate(y_intra, axis=1)

    st = st_ref[...]
    rows = [slice(c * CHUNK, (c + 1) * CHUNK) for c in range(n // CHUNK)]
    upd = [_dot_tn(bm[r], xdtw[r]) for r in rows]
    y_inter = []
    for c, r in enumerate(rows):
        y_inter.append(_dot(cm[r], st.astype(BF16)) * ecs[r])
        st = st * edl[c * CHUNK:c * CHUNK + 1, :] + upd[c]
    st_ref[...] = st
    y = y + jnp.concatenate(y_inter, axis=0) + xs * dskip_ref[...]
    yz = y * _silu(z_ref[...].astype(F32))
    yz = yz * lax.rsqrt(jnp.mean(yz * yz, axis=1, keepdims=True) + RMS_EPS) * gain_ref[...]
    o_ref[...] = yz.astype(o_ref.dtype)


def _mixer_c(p, psm, cwx, cbx, cwb, cbb, cwc, cbc, alog, dtb, dskip, gain, bsz, length):
    nl = length // SUPER
    gw, ds = C_GROUP_W, C_DSTATE

    def col(base, width):
        return lambda b, g, l: (b * nl + l, base // width + g)

    grp = lambda b, g, l: (0, g)
    vec = lambda b, g, l: (0, 0)
    return pl.pallas_call(
        _mixc_kernel,
        grid=(bsz, C_GROUPS, nl),
        in_specs=[pl.BlockSpec((SUPER, gw), col(P_ZC, gw)), pl.BlockSpec((SUPER, gw), col(P_XC, gw)),
                  pl.BlockSpec((SUPER, ds), col(P_BC, ds)), pl.BlockSpec((SUPER, ds), col(P_CC, ds)),
                  pl.BlockSpec((SUPER, LANES), lambda b, g, l: (b * nl + l, 0)),
                  pl.BlockSpec((CONV_W, gw), grp), pl.BlockSpec((1, gw), grp),
                  pl.BlockSpec((CONV_W, ds), grp), pl.BlockSpec((1, ds), grp),
                  pl.BlockSpec((CONV_W, ds), grp), pl.BlockSpec((1, ds), grp),
                  pl.BlockSpec((1, LANES), vec), pl.BlockSpec((1, LANES), vec),
                  pl.BlockSpec((1, gw), grp), pl.BlockSpec((1, gw), grp)],
        out_specs=pl.BlockSpec((SUPER, gw), lambda b, g, l: (b * nl + l, g)),
        out_shape=jax.ShapeDtypeStruct((bsz * length, C_DINNER), BF16),
        scratch_shapes=[pltpu.VMEM((ds, gw), F32),
                        pltpu.VMEM((8, gw), F32), pltpu.VMEM((8, 2 * ds), F32),
                        pltpu.VMEM((LANES, SUPER), F32)],
        compiler_params=_cparams(("parallel", "parallel", "arbitrary")),
        name="mixer_c",
    )(p, p, p, p, psm, cwx, cbx, cwb, cbb, cwc, cbc, alog, dtb, dskip, gain)


def _layernorm(y, g, b):
    mu = jnp.mean(y, axis=1, keepdims=True)
    yc = y - mu
    var = jnp.mean(yc * yc, axis=1, keepdims=True)
    return yc * lax.rsqrt(var + LN_EPS) * g + b


def _merge_kernel(oa_ref, ob_ref, oc_ref, ga_ref, gb_ref, gc_ref, woa_ref, wob_ref, woc_ref, m_ref):
    merged = jax.nn.sigmoid(ga_ref[...].astype(F32)) * _dot(oa_ref[...], woa_ref[...])
    merged = merged + jax.nn.sigmoid(gb_ref[...].astype(F32)) * _dot(ob_ref[...], wob_ref[...])
    merged = merged + jax.nn.sigmoid(gc_ref[...].astype(F32)) * _dot(oc_ref[...], woc_ref[...])
    m_ref[...] = merged.astype(m_ref.dtype)


def _merge(oa, ob, oc, p, woa, wob, woc):
    t = oa.shape[0]
    tm = min(POST_TM, t)
    d = D_MODEL
    row = lambda i: (i, 0)
    wspec = lambda shape: pl.BlockSpec(shape, lambda i: (0, 0), pipeline_mode=pl.Buffered(1))
    return pl.pallas_call(
        _merge_kernel,
        grid=(t // tm,),
        in_specs=[pl.BlockSpec((tm, 1024), row), pl.BlockSpec((tm, 1024), row), pl.BlockSpec((tm, 1024), row),
                  pl.BlockSpec((tm, d), lambda i: (i, P_GATES // d)),
                  pl.BlockSpec((tm, d), lambda i: (i, P_GATES // d + 1)),
                  pl.BlockSpec((tm, d), lambda i: (i, P_GATES // d + 2)),
                  wspec((1024, d)), wspec((1024, d)), wspec((1024, d))],
        out_specs=pl.BlockSpec((tm, d), row),
        out_shape=jax.ShapeDtypeStruct((t, d), BF16),
        compiler_params=_cparams(("parallel",)),
        name="merge",
    )(oa, ob, oc, p, p, p, woa, wob, woc)


def _route(logits, carry):
    tm = logits.shape[0]
    lane = lax.broadcasted_iota(jnp.int32, (tm, LANES), 1)
    neg = -jnp.inf
    is_g = lane < N_GROUPS
    gl = jnp.where(is_g, logits, neg)
    gmax = jnp.max(gl, axis=1, keepdims=True)
    gsum = jnp.sum(jnp.where(is_g, jnp.exp(logits - gmax), 0.0), axis=1, keepdims=True)
    p_g = 1.0 / gsum
    g_sel = jnp.min(jnp.where(gl == gmax, lane, LANES), axis=1, keepdims=True)
    in_g = (lane >= N_GROUPS) & (lane < N_GROUPS + N_EXPERTS) & (((lane - N_GROUPS) >> 2) == g_sel)
    emax = jnp.max(jnp.where(in_g, logits, neg), axis=1, keepdims=True)
    ee = jnp.where(in_g, jnp.exp(logits - emax), 0.0)
    probs = jnp.where(in_g, ee / jnp.sum(ee, axis=1, keepdims=True), -1.0)
    p1 = jnp.max(probs, axis=1, keepdims=True)
    i1 = jnp.min(jnp.where(probs == p1, lane, LANES), axis=1, keepdims=True)
    probs2 = jnp.where(lane == i1, -1.0, probs)
    p2 = jnp.max(probs2, axis=1, keepdims=True)
    i2 = jnp.min(jnp.where(probs2 == p2, lane, LANES), axis=1, keepdims=True)
    psum = p1 + p2
    w1 = p_g * (p1 / psum)
    w2 = p_g * (p2 / psum)
    l1 = (i1 - N_GROUPS) & (EXPERTS_PER_GROUP - 1)
    l2 = (i2 - N_GROUPS) & (EXPERTS_PER_GROUP - 1)
    la = jnp.minimum(l1, l2)
    lb = jnp.maximum(l1, l2)
    pair = jnp.where(la == 0, lb - 1, jnp.where(la == 1, 6 - lb, 5))
    seg = g_sel * N_PAIRS + pair
    first_is_a = l1 == jnp.where(pair < 3, PAIR_A[0], jnp.where(pair < 5, PAIR_A[3], PAIR_A[5]))
    wa = jnp.where(first_is_a, w1, w2)
    wb = jnp.where(first_is_a, w2, w1)

    oh = lane == seg
    cnt = jnp.where(oh, 1.0, 0.0)
    ri = lax.broadcasted_iota(jnp.int32, (tm, tm), 0)
    ci = lax.broadcasted_iota(jnp.int32, (tm, tm), 1)
    before = _dot(jnp.where(ci < ri, 1.0, 0.0).astype(BF16), cnt.astype(BF16)) + carry
    rank = jnp.sum(jnp.where(oh, before, 0.0), axis=1, keepdims=True).astype(jnp.int32)
    total = carry + jnp.sum(cnt, axis=0, keepdims=True)
    packed = jnp.where(lane == 0, seg, jnp.where(lane == 1, rank, jnp.zeros_like(lane)))
    return packed.T[0:8, :], jnp.where(lane == 0, wa, jnp.where(lane == 1, wb, 0.0)), total


def _post_kernel(m_ref, x_ref, wout_ref, wr_ref, lng_ref, lnb_ref, x1e_ref, ridx_ref, cnt_ref, carry_ref, y_ref):
    tm = x_ref.shape[0]
    step = pl.program_id(0)

    @pl.when(step == 0)
    def _():
        carry_ref[...] = jnp.zeros_like(carry_ref)
        y_ref[...] = jnp.zeros_like(y_ref)

    cur = step % 2
    y = y_ref[1 - cur]
    y_ref[cur] = ALPHA * x_ref[...] + _dot(m_ref[...], wout_ref[...])
    x1 = _layernorm(y, lng_ref[...], lnb_ref[...])
    x1e_ref[:, :D_MODEL] = x1

    xh = x1.astype(BF16)
    xl = (x1 - xh.astype(F32)).astype(BF16)
    prod = _dot(xh, wr_ref[...]) + _dot(xl, wr_ref[...])
    logits = prod[:, :LANES] + prod[:, LANES:]
    sub = min(ROUTE_ROWS, tm)
    total = carry_ref[0:1, :]
    for r0 in range(0, tm, sub):
        rows_t, w_lanes, total = _route(logits[r0:r0 + sub], total)
        ridx_ref[:, r0:r0 + sub] = rows_t
        x1e_ref[r0:r0 + sub, D_MODEL:] = w_lanes
    total = total * jnp.where(step > 0, 1.0, 0.0)
    carry_ref[...] = jnp.broadcast_to(total, carry_ref.shape)
    cnt_ref[...] = jnp.broadcast_to(total, cnt_ref.shape)


def _post(merged, x, wout, wr, lng, lnb):
    t = x.shape[0]
    tm = min(POST_TM, t)
    d = D_MODEL
    n = t // tm
    row_in = lambda i: (jnp.minimum(i, n - 1), 0)
    row_out = lambda i: (jnp.maximum(i - 1, 0), 0)
    const = lambda i: (0, 0)
    wspec = lambda shape: pl.BlockSpec(shape, const, pipeline_mode=pl.Buffered(1))
    return pl.pallas_call(
        _post_kernel,
        grid=(n + 1,),
        in_specs=[pl.BlockSpec((tm, d), row_in), pl.BlockSpec((tm, d), row_in),
                  wspec((d, d)), wspec((d, 2 * LANES)), wspec((1, d)), wspec((1, d))],
        out_specs=[pl.BlockSpec((tm, XE_W), row_out), pl.BlockSpec((8, tm), lambda i: (0, jnp.maximum(i - 1, 0))),
                   pl.BlockSpec((8, LANES), const)],
        out_shape=[jax.ShapeDtypeStruct((t, XE_W), F32), jax.ShapeDtypeStruct((8, t), jnp.int32),
                   jax.ShapeDtypeStruct((8, LANES), F32)],
        scratch_shapes=[pltpu.VMEM((8, LANES), F32), pltpu.VMEM((2, tm, d), F32)],
        compiler_params=_cparams(("arbitrary",)),
        name="ln_router",
    )(merged, x, wout, wr, lng, lnb)


def _row_copy(src_ref, src_row, dst_ref, dst_row, sem):
    return pltpu.make_async_copy(src_ref.at[pl.ds(src_row, 1), :], dst_ref.at[pl.ds(dst_row, 1), :], sem)


def _dispatch_kernel(slot_ref, pad_lo_ref, pad_hi_ref, x_ref, xs_ref, zbuf_ref, sem):
    tm = x_ref.shape[0]
    step = pl.program_id(0)
    base = step * tm

    @pl.when(step == 0)
    def _():
        zbuf_ref[...] = jnp.zeros_like(zbuf_ref)

        def zero_copy(lo, kk):
            return _row_copy(zbuf_ref, 0, xs_ref, lo + kk, sem)

        def bucket(sg, carry):
            lo = pad_lo_ref[sg]
            n_copies = pad_hi_ref[sg] - lo

            def start(kk, c):
                zero_copy(lo, kk).start()
                return c

            def wait(kk, c):
                zero_copy(lo, kk).wait()
                return c

            lax.fori_loop(0, n_copies, start, 0)
            lax.fori_loop(0, n_copies, wait, 0)
            return carry

        lax.fori_loop(0, N_SEG, bucket, 0)

        zr = zbuf_ref.shape[0]
        tail_lo = pad_hi_ref[N_SEG - 1]

        def tail_copy(kk):
            return pltpu.make_async_copy(zbuf_ref, xs_ref.at[pl.ds(pl.multiple_of(tail_lo + kk * zr, zr), zr), :], sem)

        def tail_start(kk, c):
            tail_copy(kk).start()
            return c

        def tail_wait(kk, c):
            tail_copy(kk).wait()
            return c

        n_tail = (xs_ref.shape[0] - tail_lo) // zr
        lax.fori_loop(0, n_tail, tail_start, 0)
        lax.fori_loop(0, n_tail, tail_wait, 0)

    def start_row(r, carry):
        _row_copy(x_ref, r, xs_ref, slot_ref[base + r], sem).start()
        return carry

    lax.fori_loop(0, tm, start_row, 0, unroll=8)
    pltpu.make_async_copy(x_ref, xs_ref.at[pl.ds(0, tm), :], sem).wait()


def _dispatch(slot, pad_lo, pad_hi, x1e, n_slots):
    t = x1e.shape[0]
    tm = min(DISPATCH_TM, t)
    return pl.pallas_call(
        _dispatch_kernel,
        grid_spec=pltpu.PrefetchScalarGridSpec(
            num_scalar_prefetch=3, grid=(t // tm,),
            in_specs=[pl.BlockSpec((tm, XE_W), lambda i, s, lo, hi: (i, 0))],
            out_specs=pl.BlockSpec(memory_space=pl.ANY),
            scratch_shapes=[pltpu.VMEM((ZERO_ROWS, XE_W), F32), pltpu.SemaphoreType.DMA(())]),
        out_shape=jax.ShapeDtypeStruct((n_slots, XE_W), F32),
        compiler_params=_cparams(("arbitrary",)),
        name="moe_dispatch",
    )(slot, pad_lo, pad_hi, x1e)


def _expert_kernel(blk_ref, ea_ref, eb_ref, valid_ref, nused_ref, x_ref, wga_ref, wua_ref, wda_ref,
                   wgb_ref, wub_ref, wdb_ref, y_ref):
    del blk_ref, ea_ref, eb_ref
    tm = x_ref.shape[0]
    half = tm // 2
    step = pl.program_id(0)
    used = step < nused_ref[0]
    full = valid_ref[step] > half

    def run(rows):
        x = x_ref[0:rows, :]
        xb = x[:, :D_MODEL].astype(BF16)
        ha = (_silu(_dot(xb, wga_ref[0])) * _dot(xb, wua_ref[0])).astype(BF16)
        hb = (_silu(_dot(xb, wgb_ref[0])) * _dot(xb, wub_ref[0])).astype(BF16)
        wa = x[:, D_MODEL:D_MODEL + 1]
        wb = x[:, D_MODEL + 1:D_MODEL + 2]
        y_ref[0:rows, :] = wa * _dot(ha, wda_ref[0]) + wb * _dot(hb, wdb_ref[0])

    @pl.when(used & full)
    def _():
        run(tm)

    @pl.when(used & jnp.logical_not(full))
    def _():
        run(half)
        y_ref[half:tm, :] = jnp.zeros((tm - half, D_MODEL), F32)

    @pl.when(jnp.logical_not(used))
    def _():
        y_ref[...] = jnp.zeros_like(y_ref)


def _experts(tile_blk, tile_ea, tile_eb, tile_valid, n_used, xs, wg, wu, wd):
    tm = MOE_TM
    n_tiles = tile_blk.shape[0]
    d, de = D_MODEL, D_EXPERT
    one = pl.Buffered(1)

    def wspec(shape, which):
        return pl.BlockSpec(shape, lambda i, tb, ea, eb, nv, nu: ((ea, eb)[which][i], 0, 0), pipeline_mode=one)

    return pl.pallas_call(
        _expert_kernel,
        grid_spec=pltpu.PrefetchScalarGridSpec(
            num_scalar_prefetch=5, grid=(n_tiles,),
            in_specs=[pl.BlockSpec((tm, XE_W), lambda i, tb, ea, eb, nv, nu: (tb[i], 0)),
                      wspec((1, d, de), 0), wspec((1, d, de), 0), wspec((1, de, d), 0),
                      wspec((1, d, de), 1), wspec((1, d, de), 1), wspec((1, de, d), 1)],
            out_specs=pl.BlockSpec((tm, d), lambda i, tb, ea, eb, nv, nu: (i, 0))),
        out_shape=jax.ShapeDtypeStruct((n_tiles * tm, d), F32),
        compiler_params=_cparams(("arbitrary",)),
        name="moe_experts",
    )(tile_blk, tile_ea, tile_eb, tile_valid, n_used, xs, wg, wu, wd, wg, wu, wd)


def _combine_kernel(slot_ref, ys_ref, x1_ref, lng_ref, lnb_ref, x2_ref, x2b_ref, ybuf_ref, sem):
    tm = x1_ref.shape[0]
    step = pl.program_id(0)

    def gather(s):
        buf = s % 2

        def start_row(r, carry):
            _row_copy(ys_ref, slot_ref[s * tm + r], ybuf_ref.at[buf], r, sem.at[buf]).start()
            return carry

        lax.fori_loop(0, tm, start_row, 0, unroll=8)

    @pl.when(step == 0)
    def _():
        gather(step)

    @pl.when(step + 1 < pl.num_programs(0))
    def _():
        gather(step + 1)

    buf = step % 2
    pltpu.make_async_copy(ys_ref.at[pl.ds(0, tm), :], ybuf_ref.at[buf], sem.at[buf]).wait()
    x2 = _layernorm(ALPHA * x1_ref[...] + ybuf_ref[buf], lng_ref[...], lnb_ref[...])
    x2_ref[...] = x2
    x2b_ref[...] = x2.astype(BF16)


def _combine(slot, ys, x1e, lng, lnb):
    t = x1e.shape[0]
    tm = min(COMBINE_TM, t)
    d = D_MODEL
    return pl.pallas_call(
        _combine_kernel,
        grid_spec=pltpu.PrefetchScalarGridSpec(
            num_scalar_prefetch=1, grid=(t // tm,),
            in_specs=[pl.BlockSpec(memory_space=pl.ANY),
                      pl.BlockSpec((tm, d), lambda i, s: (i, 0)),
                      pl.BlockSpec((1, d), lambda i, s: (0, 0)), pl.BlockSpec((1, d), lambda i, s: (0, 0))],
            out_specs=[pl.BlockSpec((tm, d), lambda i, s: (i, 0)), pl.BlockSpec((tm, d), lambda i, s: (i, 0))],
            scratch_shapes=[pltpu.VMEM((2, tm, d), F32), pltpu.SemaphoreType.DMA((2,))]),
        out_shape=[jax.ShapeDtypeStruct((t, d), F32), jax.ShapeDtypeStruct((t, d), BF16)],
        compiler_params=_cparams(("arbitrary",)),
        name="moe_combine_ln",
    )(slot, ys, x1e, lng, lnb)


def _routing_tables(ridx, cnt, t):
    tm = MOE_TM
    n_tiles = t // tm + N_SEG
    counts = cnt[0, :N_SEG].astype(jnp.int32)
    padded = ((counts + tm - 1) // tm) * tm
    ends = jnp.cumsum(padded)
    offs = ends - padded
    slot = offs[ridx[0]] + ridx[1]
    n_used = ends[-1] // tm
    tile_blk = jnp.minimum(jnp.arange(n_tiles, dtype=jnp.int32), n_used - 1)
    tile_seg = jnp.minimum(jnp.sum(tile_blk[:, None] * tm >= ends[None, :], axis=1), N_SEG - 1)
    tile_valid = jnp.clip(offs[tile_seg] + counts[tile_seg] - tile_blk * tm, 0, tm)
    first = (tile_seg // N_PAIRS) * EXPERTS_PER_GROUP
    tile_ea = first + jnp.asarray(PAIR_A, jnp.int32)[tile_seg % N_PAIRS]
    tile_eb = first + jnp.asarray(PAIR_B, jnp.int32)[tile_seg % N_PAIRS]
    i32 = lambda a: a.astype(jnp.int32)
    return (i32(slot), i32(offs + counts), i32(ends), i32(tile_blk), i32(tile_ea), i32(tile_eb), i32(tile_valid),
            i32(n_used.reshape(1)), n_tiles * tm)


def _lanes(vec, start):
    return jnp.zeros((1, LANES), F32).at[0, start:start + vec.shape[0]].set(vec.astype(F32))


def _layer_params(w_in, conv_a_w, a_log_a, dt_bias_a, norm_a, w_lr_b, b_lr_b, norm_b, conv_c_w, conv_c_b,
                  a_log_c, dt_bias_c, d_c, norm_c, w_oa, w_ob, w_oc, w_out, ln1_g, ln1_b, w_rg, w_re,
                  w_gate_e, w_up_e, w_down_e, ln2_g, ln2_b):
    def cols(a, n):
        return w_in[:, a:a + n]

    w_big = jnp.concatenate([
        cols(_GATES, 3 * D_MODEL), cols(_QA, 1024), cols(_KA, 1024), cols(_VA, 1024), cols(_ZA, 1024),
        cols(_QB, 512), cols(_KB, 512), cols(_VB, 1024), cols(_RB, 1024),
        cols(_ZC, 1024), cols(_XBC, 1536)], axis=1).astype(BF16)
    w_small = jnp.concatenate([cols(_BETA, 8), cols(_ALPHA_A, 8), cols(_LRB, 16), cols(_DTC, 16),
                               jnp.zeros((D_MODEL, LANES - 48), F32)], axis=1).astype(BF16)
    w_r = jnp.concatenate([w_rg, w_re, jnp.zeros((D_MODEL, LANES - N_GROUPS - N_EXPERTS), F32)], axis=1)
    w_rh = w_r.astype(BF16)
    w_rl = (w_r - w_rh.astype(F32)).astype(BF16)
    wlr = jnp.zeros((LANES, B_HEADS * B_DK), F32).at[S_LRB:S_LRB + B_GATE_RANK].set(w_lr_b).astype(BF16)
    row = lambda v: v.reshape(1, -1).astype(F32)
    return dict(
        w_big=w_big, w_small=w_small,
        cwq=conv_a_w[:, :1024], cwk=conv_a_w[:, 1024:2048], cwv=conv_a_w[:, 2048:],
        alog_a=_lanes(a_log_a, S_ALPHA), dtb_a=_lanes(dt_bias_a, S_ALPHA), gain_a=row(norm_a),
        wlr=wlr, blr=row(b_lr_b), gain_b=row(norm_b),
        cwx=conv_c_w[:, :C_DINNER], cbx=row(conv_c_b[:C_DINNER]),
        cwb=conv_c_w[:, C_DINNER:C_DINNER + 256], cbb=row(conv_c_b[C_DINNER:C_DINNER + 256]),
        cwc=conv_c_w[:, C_DINNER + 256:], cbc=row(conv_c_b[C_DINNER + 256:]),
        alog_c=_lanes(a_log_c, S_DT), dtb_c=_lanes(dt_bias_c, S_DT),
        dskip=row(jnp.repeat(d_c, C_HEADDIM)), gain_c=row(norm_c),
        woa=w_oa.astype(BF16), wob=w_ob.astype(BF16), woc=w_oc.astype(BF16), wout=w_out.astype(BF16),
        w_r=jnp.concatenate([w_rh, w_rl], axis=1), ln1_g=row(ln1_g), ln1_b=row(ln1_b),
        wg=w_gate_e.astype(BF16), wu=w_up_e.astype(BF16), wd=w_down_e.astype(BF16),
        ln2_g=row(ln2_g), ln2_b=row(ln2_b))


def _layer(x, xb, pr, bsz, length):
    t = bsz * length
    p, psm = _inproj(xb, pr["w_big"], pr["w_small"])
    oa = _mixer_a(p, psm, pr["cwq"], pr["cwk"], pr["cwv"], pr["alog_a"], pr["dtb_a"], pr["gain_a"], bsz, length)
    ob = _mixer_b(p, psm, pr["wlr"], pr["blr"], pr["gain_b"], bsz, length)
    oc = _mixer_c(p, psm, pr["cwx"], pr["cbx"], pr["cwb"], pr["cbb"], pr["cwc"], pr["cbc"],
                  pr["alog_c"], pr["dtb_c"], pr["dskip"], pr["gain_c"], bsz, length)
    merged = _merge(oa, ob, oc, p, pr["woa"], pr["wob"], pr["woc"])
    x1e, ridx, cnt = _post(merged, x, pr["wout"], pr["w_r"], pr["ln1_g"], pr["ln1_b"])
    slot, pad_lo, pad_hi, tile_blk, tile_ea, tile_eb, tile_valid, n_used, n_slots = _routing_tables(ridx, cnt, t)
    xs = _dispatch(slot, pad_lo, pad_hi, x1e, n_slots)
    ys = _experts(tile_blk, tile_ea, tile_eb, tile_valid, n_used, xs, pr["wg"], pr["wu"], pr["wd"])
    return _combine(slot, ys, x1e, pr["ln2_g"], pr["ln2_b"])


def kernel(x, w_in, conv_a_w, a_log_a, dt_bias_a, norm_a, w_lr_b, b_lr_b, norm_b, conv_c_w, conv_c_b, a_log_c,
           dt_bias_c, d_c, norm_c, w_oa, w_ob, w_oc, w_out, ln1_g, ln1_b, w_rg, w_re, w_gate_e, w_up_e,
           w_down_e, ln2_g, ln2_b):
    bsz, length, d = x.shape
    assert d == D_MODEL and length % SUPER == 0
    params = (w_in, conv_a_w, a_log_a, dt_bias_a, norm_a, w_lr_b, b_lr_b, norm_b, conv_c_w, conv_c_b, a_log_c,
              dt_bias_c, d_c, norm_c, w_oa, w_ob, w_oc, w_out, ln1_g, ln1_b, w_rg, w_re, w_gate_e, w_up_e,
              w_down_e, ln2_g, ln2_b)
    xf = x.reshape(bsz * length, d)
    xb = xf.astype(BF16)
    for layer in range(params[0].shape[0]):
        pr = _layer_params(*(p[layer] for p in params))
        xf, xb = _layer(xf, xb, pr, bsz, length)
    return xf.reshape(bsz, length, d)
```

```python
import functools

import jax
import jax.numpy as jnp
import numpy as np
from jax import lax
from jax.experimental import pallas as pl
from jax.experimental.pallas import tpu as pltpu

F32 = jnp.float32
BF16 = jnp.bfloat16

D_MODEL = 2048
DEPTH = 2
CHUNK = 64
CHUNK_SHIFT = 6
CONV_W = 4
A_HEADS, A_DK, A_DV = 8, 128, 128
B_HEADS, B_DK, B_DV = 4, 128, 256
B_GATE_RANK = 16
B_GATE_NORM = 16.0
C_HEADS, C_HEADDIM, C_GROUPS, C_DSTATE = 16, 64, 2, 128
C_DINNER = C_HEADS * C_HEADDIM
C_GROUP_W = C_DINNER // C_GROUPS
C_HEADS_PER_GROUP = C_HEADS // C_GROUPS
N_GROUPS, EXPERTS_PER_GROUP = 4, 4
N_EXPERTS = N_GROUPS * EXPERTS_PER_GROUP
D_EXPERT = 768
PAIR_A = (0, 0, 0, 1, 1, 3)
PAIR_B = (1, 2, 3, 3, 2, 2)
N_PAIRS = len(PAIR_A)
N_SEG = N_GROUPS * N_PAIRS
LN_EPS = 1e-5
RMS_EPS = 1e-6
ALPHA = (2 * DEPTH) ** 0.25

_QA, _KA, _VA, _ZA, _BETA, _ALPHA_A = 0, 1024, 2048, 3072, 4096, 4104
_QB, _KB, _VB, _RB, _LRB = 4112, 4624, 5136, 6160, 7184
_ZC, _XBC, _DTC, _GATES = 7200, 8224, 9760, 9776
P_GATES, P_QA, P_KA, P_VA, P_ZA = 0, 6144, 7168, 8192, 9216
P_QB, P_KB, P_VB, P_RB = 10240, 10752, 11264, 12288
P_ZC, P_XC, P_BC, P_CC = 13312, 14336, 15360, 15616
P_WIDTH = 15872
S_BETA, S_ALPHA, S_LRB, S_DT = 0, 8, 16, 32
LANES = 128
XE_W = D_MODEL + LANES
ZERO_ROWS = 64

VMEM_LIMIT_BYTES = 56 * 1024 * 1024

SUPER = 256
SUB = 128
A_HEADS_PER_STEP = 8
B_HEADS_PER_STEP = 4
PROJ_TM, PROJ_TN = 2048, 512
POST_TM = 512
ROUTE_ROWS = 128
MOE_TM = 512
DISPATCH_TM = 1024
COMBINE_TM = 512
ROW_GROUP = 8


def _cparams(sem):
    return pltpu.CompilerParams(dimension_semantics=sem, vmem_limit_bytes=VMEM_LIMIT_BYTES)


def _dot(a, b):
    return jnp.dot(a, b, preferred_element_type=F32)


def _dot_nt(a, b):
    return lax.dot_general(a, b, (((1,), (1,)), ((), ())), preferred_element_type=F32)


def _dot_tn(a, b):
    return lax.dot_general(a, b, (((0,), (0,)), ((), ())), preferred_element_type=F32)


def _split3(x):
    hi = x.astype(BF16)
    r = x - hi.astype(F32)
    mid = r.astype(BF16)
    lo = (r - mid.astype(F32)).astype(BF16)
    return hi, mid, lo


def _dot_sel(sel_bf16, x):
    hi, mid, lo = _split3(x)
    return _dot(sel_bf16, hi) + _dot(sel_bf16, mid) + _dot(sel_bf16, lo)


def _x_dot_sel(x, sel_bf16):
    hi, mid, lo = _split3(x)
    return _dot(hi, sel_bf16) + _dot(mid, sel_bf16) + _dot(lo, sel_bf16)


def _silu(x):
    return x * jax.nn.sigmoid(x)


def _softplus(x):
    return jnp.maximum(x, 0.0) + jnp.log1p(jnp.exp(-jnp.abs(x)))


def _lane_col(arr, lane_iota, idx):
    return jnp.sum(jnp.where(lane_iota == idx, arr, 0.0), axis=1, keepdims=True)


def _chunk_masks(n):
    ri = lax.broadcasted_iota(jnp.int32, (n, n), 0)
    ci = lax.broadcasted_iota(jnp.int32, (n, n), 1)
    same = (ri >> CHUNK_SHIFT) == (ci >> CHUNK_SHIFT)
    return ri, ci, same, same & (ci <= ri)


def _chunk_cumsum(causal, x):
    n = x.shape[0]
    cum = _dot_sel(jnp.where(causal, 1.0, 0.0).astype(BF16), x)
    total = jnp.concatenate(
        [jnp.broadcast_to(cum[c * CHUNK + CHUNK - 1:(c + 1) * CHUNK, :], (CHUNK, x.shape[1]))
         for c in range(n // CHUNK)], axis=0)
    return cum, total


def _shift_selector(n):
    ri = lax.broadcasted_iota(jnp.int32, ((CONV_W - 1) * n, n), 0)
    ci = lax.broadcasted_iota(jnp.int32, ((CONV_W - 1) * n, n), 1)
    s = ri // n + 1
    return jnp.where(ci == ri - (s - 1) * n - s, 1.0, 0.0).astype(BF16)


def _conv_silu(xb, w_ref, carry_ref, shift_sel, bias=None):
    n, width = xb.shape
    x = xb.astype(F32)
    shifted = _dot(shift_sel, xb)
    acc = w_ref[CONV_W - 1:CONV_W, :] * x
    prev = jnp.concatenate([carry_ref[...], jnp.zeros((8, width), F32)], axis=0)
    head = jnp.zeros((8, width), F32)
    for s in range(1, CONV_W):
        w_s = w_ref[CONV_W - 1 - s:CONV_W - s, :]
        acc = acc + w_s * shifted[(s - 1) * n:s * n]
        head = head + w_s * prev[8 - s:16 - s]
    carry_ref[...] = x[n - 8:n, :]
    acc = jnp.concatenate([acc[0:8] + head, acc[8:]], axis=0)
    if bias is not None:
        acc = acc + bias
    return _silu(acc)


def _inproj_kernel(x_ref, wb_ref, ws_ref, ob_ref, os_ref):
    x = x_ref[...]
    ob_ref[...] = _dot(x, wb_ref[...]).astype(ob_ref.dtype)

    @pl.when(pl.program_id(1) == 0)
    def _():
        os_ref[...] = _dot(x, ws_ref[...])


def _inproj(xb, w_big, w_small):
    t = xb.shape[0]
    tm = min(PROJ_TM, t)
    assert t % tm == 0 and P_WIDTH % PROJ_TN == 0
    return pl.pallas_call(
        _inproj_kernel,
        grid=(t // tm, P_WIDTH // PROJ_TN),
        in_specs=[pl.BlockSpec((tm, D_MODEL), lambda i, j: (i, 0)),
                  pl.BlockSpec((D_MODEL, PROJ_TN), lambda i, j: (0, j)),
                  pl.BlockSpec((D_MODEL, LANES), lambda i, j: (0, 0))],
        out_specs=[pl.BlockSpec((tm, PROJ_TN), lambda i, j: (i, j)),
                   pl.BlockSpec((tm, LANES), lambda i, j: (i, 0))],
        out_shape=[jax.ShapeDtypeStruct((t, P_WIDTH), BF16), jax.ShapeDtypeStruct((t, LANES), F32)],
        compiler_params=_cparams(("parallel", "arbitrary")),
        name="inproj",
    )(xb, w_big, w_small)


def _mixa_kernel(q_ref, k_ref, v_ref, z_ref, sm_ref, cwq_ref, cwk_ref, cwv_ref, alog_ref, dtb_ref, gain_ref,
                 o_ref, s_ref, cq_ref, ck_ref, cv_ref, gct_ref, *, hb):
    hg = pl.program_id(1)
    n = q_ref.shape[0]

    @pl.when(pl.program_id(2) == 0)
    def _():
        s_ref[...] = jnp.zeros_like(s_ref)
        cq_ref[...] = jnp.zeros_like(cq_ref)
        ck_ref[...] = jnp.zeros_like(ck_ref)
        cv_ref[...] = jnp.zeros_like(cv_ref)

    shift_sel = _shift_selector(n)
    q = _conv_silu(q_ref[...], cwq_ref, cq_ref, shift_sel)
    k = _conv_silu(k_ref[...], cwk_ref, ck_ref, shift_sel)
    v = _conv_silu(v_ref[...], cwv_ref, cv_ref, shift_sel)

    sm = sm_ref[...]
    lane = lax.broadcasted_iota(jnp.int32, (n, LANES), 1)
    beta_all = jax.nn.sigmoid(sm)
    g_all = -jnp.exp(alog_ref[...]) * _softplus(sm + dtb_ref[...])
    _, _, _, causal = _chunk_masks(n)
    gc, gl = _chunk_cumsum(causal, g_all)
    egc, ekd, edl = jnp.exp(gc), jnp.exp(gl - gc), jnp.exp(gl)
    gct_ref[...] = gc.T

    sub = SUB
    nb = n // sub
    assert nb == 2
    ri = lax.broadcasted_iota(jnp.int32, (sub, n), 0)
    ci = lax.broadcasted_iota(jnp.int32, (sub, n), 1) & (sub - 1)
    same = (ri >> CHUNK_SHIFT) == (ci >> CHUNK_SHIFT)
    causal_c = same & (ci <= ri)
    strict_c = same & (ci < ri)
    lev = 31 - lax.clz(ri ^ ci)
    eye = jnp.where(ri == ci, 1.0, 0.0)
    zblk = jnp.zeros((sub, sub), BF16)

    def _blockdiag(m):
        return jnp.concatenate([jnp.concatenate([m[:, :sub], zblk], axis=1),
                                jnp.concatenate([zblk, m[:, sub:]], axis=1)], axis=0)

    def _side_by_side(m):
        return jnp.concatenate([m[:sub], m[sub:]], axis=1)

    heads = range(hb)
    sls = [slice(j * A_DK, (j + 1) * A_DK) for j in heads]
    e_dls, a_s, attns, rhs, wqs, kds = [], [], [], [], [], []
    for j in heads:
        h = hg * hb + j
        qh, kh, vh = q[:, sls[j]], k[:, sls[j]], v[:, sls[j]]
        qh = qh * lax.rsqrt(jnp.sum(qh * qh, axis=1, keepdims=True) + 1e-6) * (A_DK ** -0.5)
        kh = kh * lax.rsqrt(jnp.sum(kh * kh, axis=1, keepdims=True) + 1e-6)
        beta = _lane_col(beta_all, lane, S_BETA + h)
        gcol = _lane_col(gc, lane, S_ALPHA + h)
        e_gc = _lane_col(egc, lane, S_ALPHA + h)
        e_kd = _lane_col(ekd, lane, S_ALPHA + h)
        e_dls.append(_lane_col(edl, lane, S_ALPHA + h))
        grow = gct_ref[pl.ds(S_ALPHA + h, 1), :]
        gcol_c = jnp.concatenate([jnp.broadcast_to(gcol[:sub], (sub, sub)),
                                  jnp.broadcast_to(gcol[sub:], (sub, sub))], axis=1)
        dec = jnp.exp(jnp.where(causal_c, gcol_c - grow, -jnp.inf))
        kb = kh * beta
        khb = kh.astype(BF16)
        lhs = jnp.concatenate([_side_by_side(kb), _side_by_side(qh)], axis=0).astype(BF16)
        kkqk = _dot_nt(lhs, _blockdiag(_side_by_side(khb)))
        a_s.append(jnp.where(strict_c, kkqk[:sub] * dec, 0.0))
        attns.append((kkqk[sub:] * dec).astype(BF16))
        rhs.append(jnp.concatenate([vh * beta, kb * e_gc], axis=1).astype(BF16))
        wqs.append((qh * e_gc).astype(BF16))
        kds.append((kh * e_kd).astype(BF16))
    xs = [eye - jnp.where(lev == 0, a, 0.0) for a in a_s]
    for ls in range(1, CHUNK_SHIFT):
        level = lev == ls
        xbs = [x.astype(BF16) for x in xs]
        ts = [_dot(xb, _blockdiag(jnp.where(level, a, 0.0).astype(BF16))).astype(BF16) for xb, a in zip(xbs, a_s)]
        xs = [x - _dot(t, _blockdiag(xb)) for x, t, xb in zip(xs, ts, xbs)]
    uws = []
    for x, r in zip(xs, rhs):
        xb = x.astype(BF16)
        uws.append(jnp.concatenate([_dot(xb[:, b * sub:(b + 1) * sub], r[b * sub:(b + 1) * sub])
                                    for b in range(nb)], axis=0))
    ws = [uw[:, A_DV:].astype(BF16) for uw in uws]
    ss = [s_ref[j] for j in heads]
    v_new = [[] for _ in heads]
    o_inter = [[] for _ in heads]
    for c in range(n // CHUNK):
        r = slice(c * CHUNK, (c + 1) * CHUNK)
        res = [_dot(jnp.concatenate([ws[j][r], wqs[j][r]], axis=0), ss[j].astype(BF16)) for j in heads]
        for j in heads:
            o_inter[j].append(res[j][CHUNK:])
            v_new[j].append((uws[j][r, :A_DV] - res[j][:CHUNK]).astype(BF16))
        ss = [ss[j] * e_dls[j][c * CHUNK:c * CHUNK + 1, :] + _dot_tn(kds[j][r], v_new[j][c]) for j in heads]
    for j in heads:
        s_ref[j] = ss[j]
        o_intra = _dot(attns[j], _blockdiag(_side_by_side(jnp.concatenate(v_new[j], axis=0))))
        o = jnp.concatenate(o_inter[j], axis=0) + jnp.concatenate([o_intra[:, :A_DV], o_intra[:, A_DV:]], axis=0)
        o = o * lax.rsqrt(jnp.mean(o * o, axis=1, keepdims=True) + RMS_EPS) * gain_ref[...]
        o_ref[:, sls[j]] = (o * _silu(z_ref[:, sls[j]].astype(F32))).astype(o_ref.dtype)


def _mixer_a(p, psm, cwq, cwk, cwv, alog, dtb, gain, bsz, length):
    hb = A_HEADS_PER_STEP
    w = hb * A_DK
    nl = length // SUPER

    def col(base):
        return lambda b, h, l: (b * nl + l, base // w + h)

    vec = lambda b, h, l: (0, 0)
    cw = lambda b, h, l: (0, h)
    return pl.pallas_call(
        functools.partial(_mixa_kernel, hb=hb),
        grid=(bsz, A_HEADS // hb, nl),
        in_specs=[pl.BlockSpec((SUPER, w), col(P_QA)), pl.BlockSpec((SUPER, w), col(P_KA)),
                  pl.BlockSpec((SUPER, w), col(P_VA)), pl.BlockSpec((SUPER, w), col(P_ZA)),
                  pl.BlockSpec((SUPER, LANES), lambda b, h, l: (b * nl + l, 0)),
                  pl.BlockSpec((CONV_W, w), cw), pl.BlockSpec((CONV_W, w), cw), pl.BlockSpec((CONV_W, w), cw),
                  pl.BlockSpec((1, LANES), vec), pl.BlockSpec((1, LANES), vec), pl.BlockSpec((1, A_DV), vec)],
        out_specs=pl.BlockSpec((SUPER, w), lambda b, h, l: (b * nl + l, h)),
        out_shape=jax.ShapeDtypeStruct((bsz * length, A_HEADS * A_DV), BF16),
        scratch_shapes=[pltpu.VMEM((hb, A_DK, A_DV), F32),
                        pltpu.VMEM((8, w), F32), pltpu.VMEM((8, w), F32), pltpu.VMEM((8, w), F32),
                        pltpu.VMEM((LANES, SUPER), F32)],
        compiler_params=_cparams(("parallel", "parallel", "arbitrary")),
        name="mixer_a",
    )(p, p, p, p, psm, cwq, cwk, cwv, alog, dtb, gain)


def _mixb_kernel(q_ref, k_ref, v_ref, r_ref, sm_ref, wlr_ref, blr_ref, gain_ref, o_ref, st_ref, *, hb):
    n = q_ref.shape[0]

    @pl.when(pl.program_id(2) == 0)
    def _():
        st_ref[...] = jnp.zeros_like(st_ref)

    zl = _dot(sm_ref[...].astype(BF16), wlr_ref[...]) + blr_ref[...]
    log_a = -_softplus(-zl) * (1.0 / B_GATE_NORM)
    _, _, same, causal = _chunk_masks(n)
    bc, bl = _chunk_cumsum(causal, log_a)
    q = q_ref[...].astype(F32) * (B_DK ** -0.5)
    k = k_ref[...].astype(F32)
    qe = (q * jnp.exp(bc)).astype(BF16)
    ke = (k * jnp.exp(-bc)).astype(BF16)
    kd = (k * jnp.exp(bl - bc)).astype(BF16)
    edl = jnp.exp(bl)

    heads = range(hb)
    chunks = range(n // CHUNK)
    rows = [slice(c * CHUNK, (c + 1) * CHUNK) for c in chunks]
    sls = [slice(j * B_DK, (j + 1) * B_DK) for j in heads]
    svs = [slice(j * B_DV, (j + 1) * B_DV) for j in heads]
    vhs = [v_ref[:, svs[j]] for j in heads]
    upd = [[_dot_tn(vhs[j][r], kd[r, sls[j]]) for r in rows] for j in heads]
    attn = [jnp.where(causal, _dot_nt(qe[:, sls[j]], ke[:, sls[j]]), 0.0).astype(BF16) for j in heads]
    intra = [_dot(attn[j], vhs[j]) for j in heads]
    for j in heads:
        st = st_ref[j]
        inter = []
        for c in chunks:
            inter.append(_dot_nt(qe[rows[c], sls[j]], st.astype(BF16)))
            st = st * edl[c * CHUNK:c * CHUNK + 1, sls[j]] + upd[j][c]
        st_ref[j] = st
        o = intra[j] + jnp.concatenate(inter, axis=0)
        o = o * lax.rsqrt(jnp.mean(o * o, axis=1, keepdims=True) + RMS_EPS) * gain_ref[...]
        o_ref[:, svs[j]] = (o * _silu(r_ref[:, svs[j]].astype(F32))).astype(o_ref.dtype)


def _mixer_b(p, psm, wlr, blr, gain, bsz, length):
    hb = B_HEADS_PER_STEP
    wk, wv = hb * B_DK, hb * B_DV
    nl = length // SUPER

    def col(base, width):
        return lambda b, h, l: (b * nl + l, base // width + h)

    return pl.pallas_call(
        functools.partial(_mixb_kernel, hb=hb),
        grid=(bsz, B_HEADS // hb, nl),
        in_specs=[pl.BlockSpec((SUPER, wk), col(P_QB, wk)), pl.BlockSpec((SUPER, wk), col(P_KB, wk)),
                  pl.BlockSpec((SUPER, wv), col(P_VB, wv)), pl.BlockSpec((SUPER, wv), col(P_RB, wv)),
                  pl.BlockSpec((SUPER, LANES), lambda b, h, l: (b * nl + l, 0)),
                  pl.BlockSpec((LANES, wk), lambda b, h, l: (0, h)),
                  pl.BlockSpec((1, wk), lambda b, h, l: (0, h)),
                  pl.BlockSpec((1, B_DV), lambda b, h, l: (0, 0))],
        out_specs=pl.BlockSpec((SUPER, wv), lambda b, h, l: (b * nl + l, h)),
        out_shape=jax.ShapeDtypeStruct((bsz * length, B_HEADS * B_DV), BF16),
        scratch_shapes=[pltpu.VMEM((hb, B_DV, B_DK), F32)],
        compiler_params=_cparams(("parallel", "parallel", "arbitrary")),
        name="mixer_b",
    )(p, p, p, p, psm, wlr, blr, gain)


def _mixc_kernel(z_ref, x_ref, b_ref, c_ref, sm_ref, cwx_ref, cbx_ref, cwb_ref, cbb_ref, cwc_ref, cbc_ref,
                 alog_ref, dtb_ref, dskip_ref, gain_ref, o_ref,
                 st_ref, cx_ref, cbc_ref_, cst_ref):
    g = pl.program_id(1)
    n = x_ref.shape[0]

    @pl.when(pl.program_id(2) == 0)
    def _():
        st_ref[...] = jnp.zeros_like(st_ref)
        cx_ref[...] = jnp.zeros_like(cx_ref)
        cbc_ref_[...] = jnp.zeros_like(cbc_ref_)

    shift_sel = _shift_selector(n)
    xs = _conv_silu(x_ref[...], cwx_ref, cx_ref, shift_sel, cbx_ref[...])
    bcm = _conv_silu(jnp.concatenate([b_ref[...], c_ref[...]], axis=1),
                     jnp.concatenate([cwb_ref[...], cwc_ref[...]], axis=1), cbc_ref_, shift_sel,
                     jnp.concatenate([cbb_ref[...], cbc_ref[...]], axis=1)).astype(BF16)
    bm, cm = bcm[:, :C_DSTATE], bcm[:, C_DSTATE:]

    lane = lax.broadcasted_iota(jnp.int32, (n, LANES), 1)
    dt_all = _softplus(sm_ref[...] + dtb_ref[...])
    da = dt_all * (-jnp.exp(alog_ref[...]))
    _, _, same, causal = _chunk_masks(n)
    cs, _ = _chunk_cumsum(causal, da)
    cst_ref[...] = cs.T
    er = lax.broadcasted_iota(jnp.int32, (LANES, C_GROUP_W), 0)
    ec = lax.broadcasted_iota(jnp.int32, (LANES, C_GROUP_W), 1)
    head0 = S_DT + g * C_HEADS_PER_GROUP
    expand = jnp.where(er == head0 + ec // C_HEADDIM, 1.0, 0.0).astype(BF16)
    dt_e = _dot(dt_all.astype(BF16), expand)
    cs_e = _x_dot_sel(cs, expand)
    csl_e = jnp.concatenate(
        [jnp.broadcast_to(cs_e[c * CHUNK + CHUNK - 1:(c + 1) * CHUNK, :], (CHUNK, C_GROUP_W))
         for c in range(n // CHUNK)], axis=0)
    ecs, edl = jnp.exp(cs_e), jnp.exp(csl_e)
    xdt = xs * dt_e
    xdt_b = xdt.astype(BF16)
    xdtw = (xdt * jnp.exp(csl_e - cs_e)).astype(BF16)
    sub = SUB
    assert n == 2 * sub
    ri = lax.broadcasted_iota(jnp.int32, (sub, n), 0)
    ci = lax.broadcasted_iota(jnp.int32, (sub, n), 1) & (sub - 1)
    causal_c = ((ri >> CHUNK_SHIFT) == (ci >> CHUNK_SHIFT)) & (ci <= ri)
    zblk = jnp.zeros((sub, LANES), BF16)

    def _blockdiag(top, bottom):
        return jnp.concatenate([jnp.concatenate([top, zblk], axis=1),
                                jnp.concatenate([zblk, bottom], axis=1)], axis=0)

    scores = jnp.where(causal_c, _dot_nt(jnp.concatenate([cm[:sub], cm[sub:]], axis=1),
                                         _blockdiag(bm[:sub], bm[sub:])), 0.0)
    half = lax.broadcasted_iota(jnp.int32, (n, LANES), 1) // C_HEADDIM
    y_intra = []
    for pair in range(C_HEADS_PER_GROUP // 2):
        xp = xdt_b[:, pair * LANES:(pair + 1) * LANES]
        acc = None
        for side in range(2):
            hidx = head0 + 2 * pair + side
            ccol = _lane_col(cs, lane, hidx)
            ccol_c = jnp.concatenate([jnp.broadcast_to(ccol[:sub], (sub, sub)),
                                      jnp.broadcast_to(ccol[sub:], (sub, sub))], axis=1)
            crow = cst_ref[pl.ds(hidx, 1), :]
            dec = jnp.exp(jnp.where(causal_c, ccol_c - crow, -jnp.inf))
            xm = jnp.where(half == side, xp, jnp.zeros_like(xp))
            part = _dot((scores * dec).astype(BF16), _blockdiag(xm[:sub], xm[sub:]))
            acc = part if acc is None else acc + part
        y_intra.append(jnp.concatenate([acc[:, :LANES], acc[:, LANES:]], axis=0))
    y = jnp.concatenate(y_intra, axis=1)

    st = st_ref[...]
    rows = [slice(c * CHUNK, (c + 1) * CHUNK) for c in range(n // CHUNK)]
    upd = [_dot_tn(bm[r], xdtw[r]) for r in rows]
    y_inter = []
    for c, r in enumerate(rows):
        y_inter.append(_dot(cm[r], st.astype(BF16)) * ecs[r])
        st = st * edl[c * CHUNK:c * CHUNK + 1, :] + upd[c]
    st_ref[...] = st
    y = y + jnp.concatenate(y_inter, axis=0) + xs * dskip_ref[...]
    yz = y * _silu(z_ref[...].astype(F32))
    yz = yz * lax.rsqrt(jnp.mean(yz * yz, axis=1, keepdims=True) + RMS_EPS) * gain_ref[...]
    o_ref[...] = yz.astype(o_ref.dtype)


def _mixer_c(p, psm, cwx, cbx, cwb, cbb, cwc, cbc, alog, dtb, dskip, gain, bsz, length):
    nl = length // SUPER
    gw, ds = C_GROUP_W, C_DSTATE

    def col(base, width):
        return lambda b, g, l: (b * nl + l, base // width + g)

    grp = lambda b, g, l: (0, g)
    vec = lambda b, g, l: (0, 0)
    return pl.pallas_call(
        _mixc_kernel,
        grid=(bsz, C_GROUPS, nl),
        in_specs=[pl.BlockSpec((SUPER, gw), col(P_ZC, gw)), pl.BlockSpec((SUPER, gw), col(P_XC, gw)),
                  pl.BlockSpec((SUPER, ds), col(P_BC, ds)), pl.BlockSpec((SUPER, ds), col(P_CC, ds)),
                  pl.BlockSpec((SUPER, LANES), lambda b, g, l: (b * nl + l, 0)),
                  pl.BlockSpec((CONV_W, gw), grp), pl.BlockSpec((1, gw), grp),
                  pl.BlockSpec((CONV_W, ds), grp), pl.BlockSpec((1, ds), grp),
                  pl.BlockSpec((CONV_W, ds), grp), pl.BlockSpec((1, ds), grp),
                  pl.BlockSpec((1, LANES), vec), pl.BlockSpec((1, LANES), vec),
                  pl.BlockSpec((1, gw), grp), pl.BlockSpec((1, gw), grp)],
        out_specs=pl.BlockSpec((SUPER, gw), lambda b, g, l: (b * nl + l, g)),
        out_shape=jax.ShapeDtypeStruct((bsz * length, C_DINNER), BF16),
        scratch_shapes=[pltpu.VMEM((ds, gw), F32),
                        pltpu.VMEM((8, gw), F32), pltpu.VMEM((8, 2 * ds), F32),
                        pltpu.VMEM((LANES, SUPER), F32)],
        compiler_params=_cparams(("parallel", "parallel", "arbitrary")),
        name="mixer_c",
    )(p, p, p, p, psm, cwx, cbx, cwb, cbb, cwc, cbc, alog, dtb, dskip, gain)


def _layernorm(y, g, b):
    mu = jnp.mean(y, axis=1, keepdims=True)
    yc = y - mu
    var = jnp.mean(yc * yc, axis=1, keepdims=True)
    return yc * lax.rsqrt(var + LN_EPS) * g + b


def _merge_kernel(oa_ref, ob_ref, oc_ref, ga_ref, gb_ref, gc_ref, woa_ref, wob_ref, woc_ref, m_ref):
    merged = jax.nn.sigmoid(ga_ref[...].astype(F32)) * _dot(oa_ref[...], woa_ref[...])
    merged = merged + jax.nn.sigmoid(gb_ref[...].astype(F32)) * _dot(ob_ref[...], wob_ref[...])
    merged = merged + jax.nn.sigmoid(gc_ref[...].astype(F32)) * _dot(oc_ref[...], woc_ref[...])
    m_ref[...] = merged.astype(m_ref.dtype)


def _merge(oa, ob, oc, p, woa, wob, woc):
    t = oa.shape[0]
    tm = min(POST_TM, t)
    d = D_MODEL
    row = lambda i: (i, 0)
    wspec = lambda shape: pl.BlockSpec(shape, lambda i: (0, 0), pipeline_mode=pl.Buffered(1))
    return pl.pallas_call(
        _merge_kernel,
        grid=(t // tm,),
        in_specs=[pl.BlockSpec((tm, 1024), row), pl.BlockSpec((tm, 1024), row), pl.BlockSpec((tm, 1024), row),
                  pl.BlockSpec((tm, d), lambda i: (i, P_GATES // d)),
                  pl.BlockSpec((tm, d), lambda i: (i, P_GATES // d + 1)),
                  pl.BlockSpec((tm, d), lambda i: (i, P_GATES // d + 2)),
                  wspec((1024, d)), wspec((1024, d)), wspec((1024, d))],
        out_specs=pl.BlockSpec((tm, d), row),
        out_shape=jax.ShapeDtypeStruct((t, d), BF16),
        compiler_params=_cparams(("parallel",)),
        name="merge",
    )(oa, ob, oc, p, p, p, woa, wob, woc)


def _route(logits, carry):
    tm = logits.shape[0]
    lane = lax.broadcasted_iota(jnp.int32, (tm, LANES), 1)
    neg = -jnp.inf
    is_g = lane < N_GROUPS
    gl = jnp.where(is_g, logits, neg)
    gmax = jnp.max(gl, axis=1, keepdims=True)
    gsum = jnp.sum(jnp.where(is_g, jnp.exp(logits - gmax), 0.0), axis=1, keepdims=True)
    p_g = 1.0 / gsum
    g_sel = jnp.min(jnp.where(gl == gmax, lane, LANES), axis=1, keepdims=True)
    in_g = (lane >= N_GROUPS) & (lane < N_GROUPS + N_EXPERTS) & (((lane - N_GROUPS) >> 2) == g_sel)
    emax = jnp.max(jnp.where(in_g, logits, neg), axis=1, keepdims=True)
    ee = jnp.where(in_g, jnp.exp(logits - emax), 0.0)
    probs = jnp.where(in_g, ee / jnp.sum(ee, axis=1, keepdims=True), -1.0)
    p1 = jnp.max(probs, axis=1, keepdims=True)
    i1 = jnp.min(jnp.where(probs == p1, lane, LANES), axis=1, keepdims=True)
    probs2 = jnp.where(lane == i1, -1.0, probs)
    p2 = jnp.max(probs2, axis=1, keepdims=True)
    i2 = jnp.min(jnp.where(probs2 == p2, lane, LANES), axis=1, keepdims=True)
    psum = p1 + p2
    w1 = p_g * (p1 / psum)
    w2 = p_g * (p2 / psum)
    l1 = (i1 - N_GROUPS) & (EXPERTS_PER_GROUP - 1)
    l2 = (i2 - N_GROUPS) & (EXPERTS_PER_GROUP - 1)
    la = jnp.minimum(l1, l2)
    lb = jnp.maximum(l1, l2)
    pair = jnp.where(la == 0, lb - 1, jnp.where(la == 1, 6 - lb, 5))
    seg = g_sel * N_PAIRS + pair
    first_is_a = l1 == jnp.where(pair < 3, PAIR_A[0], jnp.where(pair < 5, PAIR_A[3], PAIR_A[5]))
    wa = jnp.where(first_is_a, w1, w2)
    wb = jnp.where(first_is_a, w2, w1)

    oh = lane == seg
    cnt = jnp.where(oh, 1.0, 0.0)
    ri = lax.broadcasted_iota(jnp.int32, (tm, tm), 0)
    ci = lax.broadcasted_iota(jnp.int32, (tm, tm), 1)
    before = _dot(jnp.where(ci < ri, 1.0, 0.0).astype(BF16), cnt.astype(BF16)) + carry
    rank = jnp.sum(jnp.where(oh, before, 0.0), axis=1, keepdims=True).astype(jnp.int32)
    total = carry + jnp.sum(cnt, axis=0, keepdims=True)
    packed = jnp.where(lane == 0, seg, jnp.where(lane == 1, rank, jnp.zeros_like(lane)))
    return packed.T[0:8, :], jnp.where(lane == 0, wa, jnp.where(lane == 1, wb, 0.0)), total


def _post_kernel(m_ref, x_ref, wout_ref, wr_ref, lng_ref, lnb_ref, x1e_ref, ridx_ref, cnt_ref, carry_ref, y_ref):
    tm = x_ref.shape[0]
    step = pl.program_id(0)

    @pl.when(step == 0)
    def _():
        carry_ref[...] = jnp.zeros_like(carry_ref)
        y_ref[...] = jnp.zeros_like(y_ref)

    cur = step % 2
    y = y_ref[1 - cur]
    y_ref[cur] = ALPHA * x_ref[...] + _dot(m_ref[...], wout_ref[...])
    x1 = _layernorm(y, lng_ref[...], lnb_ref[...])
    x1e_ref[:, :D_MODEL] = x1

    xh = x1.astype(BF16)
    xl = (x1 - xh.astype(F32)).astype(BF16)
    prod = _dot(xh, wr_ref[...]) + _dot(xl, wr_ref[...])
    logits = prod[:, :LANES] + prod[:, LANES:]
    sub = min(ROUTE_ROWS, tm)
    total = carry_ref[0:1, :]
    for r0 in range(0, tm, sub):
        rows_t, w_lanes, total = _route(logits[r0:r0 + sub], total)
        ridx_ref[:, r0:r0 + sub] = rows_t
        x1e_ref[r0:r0 + sub, D_MODEL:] = w_lanes
    total = total * jnp.where(step > 0, 1.0, 0.0)
    carry_ref[...] = jnp.broadcast_to(total, carry_ref.shape)
    cnt_ref[...] = jnp.broadcast_to(total, cnt_ref.shape)


def _post(merged, x, wout, wr, lng, lnb):
    t = x.shape[0]
    tm = min(POST_TM, t)
    d = D_MODEL
    n = t // tm
    row_in = lambda i: (jnp.minimum(i, n - 1), 0)
    row_out = lambda i: (jnp.maximum(i - 1, 0), 0)
    const = lambda i: (0, 0)
    wspec = lambda shape: pl.BlockSpec(shape, const, pipeline_mode=pl.Buffered(1))
    return pl.pallas_call(
        _post_kernel,
        grid=(n + 1,),
        in_specs=[pl.BlockSpec((tm, d), row_in), pl.BlockSpec((tm, d), row_in),
                  wspec((d, d)), wspec((d, 2 * LANES)), wspec((1, d)), wspec((1, d))],
        out_specs=[pl.BlockSpec((tm, XE_W), row_out), pl.BlockSpec((8, tm), lambda i: (0, jnp.maximum(i - 1, 0))),
                   pl.BlockSpec((8, LANES), const)],
        out_shape=[jax.ShapeDtypeStruct((t, XE_W), F32), jax.ShapeDtypeStruct((8, t), jnp.int32),
                   jax.ShapeDtypeStruct((8, LANES), F32)],
        scratch_shapes=[pltpu.VMEM((8, LANES), F32), pltpu.VMEM((2, tm, d), F32)],
        compiler_params=_cparams(("arbitrary",)),
        name="ln_router",
    )(merged, x, wout, wr, lng, lnb)


def _row_copy(src_ref, src_row, dst_ref, dst_row, sem):
    return pltpu.make_async_copy(src_ref.at[pl.ds(src_row, 1), :], dst_ref.at[pl.ds(dst_row, 1), :], sem)


def _dispatch_kernel(slot_ref, pad_lo_ref, pad_hi_ref, x_ref, xs_ref, zbuf_ref, sem):
    tm = x_ref.shape[0] * ROW_GROUP
    step = pl.program_id(0)
    base = step * tm

    @pl.when(step == 0)
    def _():
        zbuf_ref[...] = jnp.zeros_like(zbuf_ref)

        def zero_copy(lo, kk):
            return _row_copy(zbuf_ref, 0, xs_ref, lo + kk, sem)

        def bucket(sg, carry):
            lo = pad_lo_ref[sg]
            n_copies = pad_hi_ref[sg] - lo

            def start(kk, c):
                zero_copy(lo, kk).start()
                return c

            def wait(kk, c):
                zero_copy(lo, kk).wait()
                return c

            lax.fori_loop(0, n_copies, start, 0)
            lax.fori_loop(0, n_copies, wait, 0)
            return carry

        lax.fori_loop(0, N_SEG, bucket, 0)

        zr = zbuf_ref.shape[0]
        tail_lo = pad_hi_ref[N_SEG - 1]

        def tail_copy(kk):
            return pltpu.make_async_copy(zbuf_ref, xs_ref.at[pl.ds(pl.multiple_of(tail_lo + kk * zr, zr), zr), :], sem)

        def tail_start(kk, c):
            tail_copy(kk).start()
            return c

        def tail_wait(kk, c):
            tail_copy(kk).wait()
            return c

        n_tail = (xs_ref.shape[0] - tail_lo) // zr
        lax.fori_loop(0, n_tail, tail_start, 0)
        lax.fori_loop(0, n_tail, tail_wait, 0)

    def start_rows(g, carry):
        r0 = pl.multiple_of(g * ROW_GROUP, ROW_GROUP)
        for u in range(ROW_GROUP):
            pltpu.make_async_copy(x_ref.at[g, pl.ds(u, 1), :],
                                  xs_ref.at[pl.ds(slot_ref[base + r0 + u], 1), :], sem).start()
        return carry

    lax.fori_loop(0, tm // ROW_GROUP, start_rows, 0)
    pltpu.make_async_copy(xs_ref.at[pl.ds(0, tm), :], xs_ref.at[pl.ds(0, tm), :], sem).wait()


def _dispatch(slot, pad_lo, pad_hi, x1e, n_slots):
    t = x1e.shape[0]
    tm = min(DISPATCH_TM, t)
    return pl.pallas_call(
        _dispatch_kernel,
        grid_spec=pltpu.PrefetchScalarGridSpec(
            num_scalar_prefetch=3, grid=(t // tm,),
            in_specs=[pl.BlockSpec((tm // ROW_GROUP, ROW_GROUP, XE_W), lambda i, s, lo, hi: (i, 0, 0))],
            out_specs=pl.BlockSpec(memory_space=pl.ANY),
            scratch_shapes=[pltpu.VMEM((ZERO_ROWS, XE_W), F32), pltpu.SemaphoreType.DMA(())]),
        out_shape=jax.ShapeDtypeStruct((n_slots, XE_W), F32),
        compiler_params=_cparams(("arbitrary",)),
        name="moe_dispatch",
    )(slot, pad_lo, pad_hi, x1e.reshape(t // ROW_GROUP, ROW_GROUP, XE_W))


def _expert_kernel(blk_ref, ea_ref, eb_ref, valid_ref, nused_ref, x_ref, wga_ref, wua_ref, wda_ref,
                   wgb_ref, wub_ref, wdb_ref, y_ref):
    del blk_ref, ea_ref, eb_ref
    tm = x_ref.shape[0]
    half = tm // 2
    step = pl.program_id(0)
    used = step < nused_ref[0]
    full = valid_ref[step] > half

    def run(rows):
        x = x_ref[0:rows, :]
        xb = x[:, :D_MODEL].astype(BF16)
        ha = (_silu(_dot(xb, wga_ref[0])) * _dot(xb, wua_ref[0])).astype(BF16)
        hb = (_silu(_dot(xb, wgb_ref[0])) * _dot(xb, wub_ref[0])).astype(BF16)
        wa = x[:, D_MODEL:D_MODEL + 1]
        wb = x[:, D_MODEL + 1:D_MODEL + 2]
        y_ref[0:rows, :] = wa * _dot(ha, wda_ref[0]) + wb * _dot(hb, wdb_ref[0])

    @pl.when(used & full)
    def _():
        run(tm)

    @pl.when(used & jnp.logical_not(full))
    def _():
        run(half)
        y_ref[half:tm, :] = jnp.zeros((tm - half, D_MODEL), F32)

    @pl.when(jnp.logical_not(used))
    def _():
        y_ref[...] = jnp.zeros_like(y_ref)


def _experts(tile_blk, tile_ea, tile_eb, tile_valid, n_used, xs, wg, wu, wd):
    tm = MOE_TM
    n_tiles = tile_blk.shape[0]
    d, de = D_MODEL, D_EXPERT
    one = pl.Buffered(1)

    def wspec(shape, which):
        return pl.BlockSpec(shape, lambda i, tb, ea, eb, nv, nu: ((ea, eb)[which][i], 0, 0), pipeline_mode=one)

    return pl.pallas_call(
        _expert_kernel,
        grid_spec=pltpu.PrefetchScalarGridSpec(
            num_scalar_prefetch=5, grid=(n_tiles,),
            in_specs=[pl.BlockSpec((tm, XE_W), lambda i, tb, ea, eb, nv, nu: (tb[i], 0)),
                      wspec((1, d, de), 0), wspec((1, d, de), 0), wspec((1, de, d), 0),
                      wspec((1, d, de), 1), wspec((1, d, de), 1), wspec((1, de, d), 1)],
            out_specs=pl.BlockSpec((tm, d), lambda i, tb, ea, eb, nv, nu: (i, 0))),
        out_shape=jax.ShapeDtypeStruct((n_tiles * tm, d), F32),
        compiler_params=_cparams(("arbitrary",)),
        name="moe_experts",
    )(tile_blk, tile_ea, tile_eb, tile_valid, n_used, xs, wg, wu, wd, wg, wu, wd)


def _combine_kernel(slot_ref, ys_ref, x1_ref, lng_ref, lnb_ref, x2_ref, x2b_ref, ybuf_ref, sem):
    tm = x1_ref.shape[0]
    step = pl.program_id(0)

    def gather(s):
        buf = s % 2

        def start_rows(g, carry):
            r0 = pl.multiple_of(g * ROW_GROUP, ROW_GROUP)
            for u in range(ROW_GROUP):
                pltpu.make_async_copy(ys_ref.at[pl.ds(slot_ref[s * tm + r0 + u], 1), :],
                                      ybuf_ref.at[buf, g, pl.ds(u, 1), :], sem.at[buf]).start()
            return carry

        lax.fori_loop(0, tm // ROW_GROUP, start_rows, 0)

    @pl.when(step == 0)
    def _():
        gather(step)

    @pl.when(step + 1 < pl.num_programs(0))
    def _():
        gather(step + 1)

    buf = step % 2
    pltpu.make_async_copy(ys_ref.at[pl.ds(0, tm), :], ys_ref.at[pl.ds(0, tm), :], sem.at[buf]).wait()
    y = ybuf_ref[buf].reshape(tm, D_MODEL)
    x2 = _layernorm(ALPHA * x1_ref[...] + y, lng_ref[...], lnb_ref[...])
    x2_ref[...] = x2
    x2b_ref[...] = x2.astype(BF16)


def _combine(slot, ys, x1e, lng, lnb):
    t = x1e.shape[0]
    tm = min(COMBINE_TM, t)
    d = D_MODEL
    return pl.pallas_call(
        _combine_kernel,
        grid_spec=pltpu.PrefetchScalarGridSpec(
            num_scalar_prefetch=1, grid=(t // tm,),
            in_specs=[pl.BlockSpec(memory_space=pl.ANY),
                      pl.BlockSpec((tm, d), lambda i, s: (i, 0)),
                      pl.BlockSpec((1, d), lambda i, s: (0, 0)), pl.BlockSpec((1, d), lambda i, s: (0, 0))],
            out_specs=[pl.BlockSpec((tm, d), lambda i, s: (i, 0)), pl.BlockSpec((tm, d), lambda i, s: (i, 0))],
            scratch_shapes=[pltpu.VMEM((2, tm // ROW_GROUP, ROW_GROUP, d), F32), pltpu.SemaphoreType.DMA((2,))]),
        out_shape=[jax.ShapeDtypeStruct((t, d), F32), jax.ShapeDtypeStruct((t, d), BF16)],
        compiler_params=_cparams(("arbitrary",)),
        name="moe_combine_ln",
    )(slot, ys, x1e, lng, lnb)


def _routing_tables(ridx, cnt, t):
    tm = MOE_TM
    n_tiles = t // tm + N_SEG
    counts = cnt[0, :N_SEG].astype(jnp.int32)
    padded = ((counts + tm - 1) // tm) * tm
    ends = jnp.cumsum(padded)
    offs = ends - padded
    slot = offs[ridx[0]] + ridx[1]
    n_used = ends[-1] // tm
    tile_blk = jnp.minimum(jnp.arange(n_tiles, dtype=jnp.int32), n_used - 1)
    tile_seg = jnp.minimum(jnp.sum(tile_blk[:, None] * tm >= ends[None, :], axis=1), N_SEG - 1)
    tile_valid = jnp.clip(offs[tile_seg] + counts[tile_seg] - tile_blk * tm, 0, tm)
    first = (tile_seg // N_PAIRS) * EXPERTS_PER_GROUP
    tile_ea = first + jnp.asarray(PAIR_A, jnp.int32)[tile_seg % N_PAIRS]
    tile_eb = first + jnp.asarray(PAIR_B, jnp.int32)[tile_seg % N_PAIRS]
    i32 = lambda a: a.astype(jnp.int32)
    return (i32(slot), i32(offs + counts), i32(ends), i32(tile_blk), i32(tile_ea), i32(tile_eb), i32(tile_valid),
            i32(n_used.reshape(1)), n_tiles * tm)


def _lanes(vec, start):
    return jnp.zeros((1, LANES), F32).at[0, start:start + vec.shape[0]].set(vec.astype(F32))


def _layer_params(w_in, conv_a_w, a_log_a, dt_bias_a, norm_a, w_lr_b, b_lr_b, norm_b, conv_c_w, conv_c_b,
                  a_log_c, dt_bias_c, d_c, norm_c, w_oa, w_ob, w_oc, w_out, ln1_g, ln1_b, w_rg, w_re,
                  w_gate_e, w_up_e, w_down_e, ln2_g, ln2_b):
    def cols(a, n):
        return w_in[:, a:a + n]

    w_big = jnp.concatenate([
        cols(_GATES, 3 * D_MODEL), cols(_QA, 1024), cols(_KA, 1024), cols(_VA, 1024), cols(_ZA, 1024),
        cols(_QB, 512), cols(_KB, 512), cols(_VB, 1024), cols(_RB, 1024),
        cols(_ZC, 1024), cols(_XBC, 1536)], axis=1).astype(BF16)
    w_small = jnp.concatenate([cols(_BETA, 8), cols(_ALPHA_A, 8), cols(_LRB, 16), cols(_DTC, 16),
                               jnp.zeros((D_MODEL, LANES - 48), F32)], axis=1).astype(BF16)
    w_r = jnp.concatenate([w_rg, w_re, jnp.zeros((D_MODEL, LANES - N_GROUPS - N_EXPERTS), F32)], axis=1)
    w_rh = w_r.astype(BF16)
    w_rl = (w_r - w_rh.astype(F32)).astype(BF16)
    wlr = jnp.zeros((LANES, B_HEADS * B_DK), F32).at[S_LRB:S_LRB + B_GATE_RANK].set(w_lr_b).astype(BF16)
    row = lambda v: v.reshape(1, -1).astype(F32)
    return dict(
        w_big=w_big, w_small=w_small,
        cwq=conv_a_w[:, :1024], cwk=conv_a_w[:, 1024:2048], cwv=conv_a_w[:, 2048:],
        alog_a=_lanes(a_log_a, S_ALPHA), dtb_a=_lanes(dt_bias_a, S_ALPHA), gain_a=row(norm_a),
        wlr=wlr, blr=row(b_lr_b), gain_b=row(norm_b),
        cwx=conv_c_w[:, :C_DINNER], cbx=row(conv_c_b[:C_DINNER]),
        cwb=conv_c_w[:, C_DINNER:C_DINNER + 256], cbb=row(conv_c_b[C_DINNER:C_DINNER + 256]),
        cwc=conv_c_w[:, C_DINNER + 256:], cbc=row(conv_c_b[C_DINNER + 256:]),
        alog_c=_lanes(a_log_c, S_DT), dtb_c=_lanes(dt_bias_c, S_DT),
        dskip=row(jnp.repeat(d_c, C_HEADDIM)), gain_c=row(norm_c),
        woa=w_oa.astype(BF16), wob=w_ob.astype(BF16), woc=w_oc.astype(BF16), wout=w_out.astype(BF16),
        w_r=jnp.concatenate([w_rh, w_rl], axis=1), ln1_g=row(ln1_g), ln1_b=row(ln1_b),
        wg=w_gate_e.astype(BF16), wu=w_up_e.astype(BF16), wd=w_down_e.astype(BF16),
        ln2_g=row(ln2_g), ln2_b=row(ln2_b))


def _layer(x, xb, pr, bsz, length):
    t = bsz * length
    p, psm = _inproj(xb, pr["w_big"], pr["w_small"])
    oa = _mixer_a(p, psm, pr["cwq"], pr["cwk"], pr["cwv"], pr["alog_a"], pr["dtb_a"], pr["gain_a"], bsz, length)
    ob = _mixer_b(p, psm, pr["wlr"], pr["blr"], pr["gain_b"], bsz, length)
    oc = _mixer_c(p, psm, pr["cwx"], pr["cbx"], pr["cwb"], pr["cbb"], pr["cwc"], pr["cbc"],
                  pr["alog_c"], pr["dtb_c"], pr["dskip"], pr["gain_c"], bsz, length)
    merged = _merge(oa, ob, oc, p, pr["woa"], pr["wob"], pr["woc"])
    x1e, ridx, cnt = _post(merged, x, pr["wout"], pr["w_r"], pr["ln1_g"], pr["ln1_b"])
    slot, pad_lo, pad_hi, tile_blk, tile_ea, tile_eb, tile_valid, n_used, n_slots = _routing_tables(ridx, cnt, t)
    xs = _dispatch(slot, pad_lo, pad_hi, x1e, n_slots)
    ys = _experts(tile_blk, tile_ea, tile_eb, tile_valid, n_used, xs, pr["wg"], pr["wu"], pr["wd"])
    return _combine(slot, ys, x1e, pr["ln2_g"], pr["ln2_b"])


def kernel(x, w_in, conv_a_w, a_log_a, dt_bias_a, norm_a, w_lr_b, b_lr_b, norm_b, conv_c_w, conv_c_b, a_log_c,
           dt_bias_c, d_c, norm_c, w_oa, w_ob, w_oc, w_out, ln1_g, ln1_b, w_rg, w_re, w_gate_e, w_up_e,
           w_down_e, ln2_g, ln2_b):
    bsz, length, d = x.shape
    assert d == D_MODEL and length % SUPER == 0
    params = (w_in, conv_a_w, a_log_a, dt_bias_a, norm_a, w_lr_b, b_lr_b, norm_b, conv_c_w, conv_c_b, a_log_c,
              dt_bias_c, d_c, norm_c, w_oa, w_ob, w_oc, w_out, ln1_g, ln1_b, w_rg, w_re, w_gate_e, w_up_e,
              w_down_e, ln2_g, ln2_b)
    xf = x.reshape(bsz * length, d)
    xb = xf.astype(BF16)
    for layer in range(params[0].shape[0]):
        pr = _layer_params(*(p[layer] for p in params))
        xf, xb = _layer(xf, xb, pr, bsz, length)
    return xf.reshape(bsz, length, d)
```

```python
import functools

import jax
import jax.numpy as jnp
import numpy as np
from jax import lax
from jax.experimental import pallas as pl
from jax.experimental.pallas import tpu as pltpu

F32 = jnp.float32
BF16 = jnp.bfloat16

D_MODEL = 2048
DEPTH = 2
CHUNK = 64
CHUNK_SHIFT = 6
CONV_W = 4
A_HEADS, A_DK, A_DV = 8, 128, 128
B_HEADS, B_DK, B_DV = 4, 128, 256
B_GATE_RANK = 16
B_GATE_NORM = 16.0
C_HEADS, C_HEADDIM, C_GROUPS, C_DSTATE = 16, 64, 2, 128
C_DINNER = C_HEADS * C_HEADDIM
C_GROUP_W = C_DINNER // C_GROUPS
C_HEADS_PER_GROUP = C_HEADS // C_GROUPS
N_GROUPS, EXPERTS_PER_GROUP = 4, 4
N_EXPERTS = N_GROUPS * EXPERTS_PER_GROUP
D_EXPERT = 768
PAIR_A = (0, 0, 0, 1, 1, 3)
PAIR_B = (1, 2, 3, 3, 2, 2)
N_PAIRS = len(PAIR_A)
N_SEG = N_GROUPS * N_PAIRS
LN_EPS = 1e-5
RMS_EPS = 1e-6
ALPHA = (2 * DEPTH) ** 0.25

_QA, _KA, _VA, _ZA, _BETA, _ALPHA_A = 0, 1024, 2048, 3072, 4096, 4104
_QB, _KB, _VB, _RB, _LRB = 4112, 4624, 5136, 6160, 7184
_ZC, _XBC, _DTC, _GATES = 7200, 8224, 9760, 9776
P_GATES, P_QA, P_KA, P_VA, P_ZA = 0, 6144, 7168, 8192, 9216
P_QB, P_KB, P_VB, P_RB = 10240, 10752, 11264, 12288
P_ZC, P_XC, P_BC, P_CC = 13312, 14336, 15360, 15616
P_WIDTH = 15872
S_BETA, S_ALPHA, S_LRB, S_DT = 0, 8, 16, 32
LANES = 128
XE_W = D_MODEL + LANES
ZERO_ROWS = 64

VMEM_LIMIT_BYTES = 56 * 1024 * 1024
EXPERT_VMEM_LIMIT_BYTES = 62 * 1024 * 1024

SUPER = 256
SUB = 128
A_HEADS_PER_STEP = 8
B_HEADS_PER_STEP = 4
PROJ_TM, PROJ_TN = 2048, 512
POST_TM = 512
ROUTE_ROWS = 128
MOE_TM = 512
DISPATCH_TM = 1024
COMBINE_TM = 512
ROW_GROUP = 8


def _cparams(sem, vmem_limit_bytes=VMEM_LIMIT_BYTES):
    return pltpu.CompilerParams(dimension_semantics=sem, vmem_limit_bytes=vmem_limit_bytes)


def _dot(a, b):
    return jnp.dot(a, b, preferred_element_type=F32)


def _dot_nt(a, b):
    return lax.dot_general(a, b, (((1,), (1,)), ((), ())), preferred_element_type=F32)


def _dot_tn(a, b):
    return lax.dot_general(a, b, (((0,), (0,)), ((), ())), preferred_element_type=F32)


def _split3(x):
    hi = x.astype(BF16)
    r = x - hi.astype(F32)
    mid = r.astype(BF16)
    lo = (r - mid.astype(F32)).astype(BF16)
    return hi, mid, lo


def _dot_sel(sel_bf16, x):
    hi, mid, lo = _split3(x)
    return _dot(sel_bf16, hi) + _dot(sel_bf16, mid) + _dot(sel_bf16, lo)


def _x_dot_sel(x, sel_bf16):
    hi, mid, lo = _split3(x)
    return _dot(hi, sel_bf16) + _dot(mid, sel_bf16) + _dot(lo, sel_bf16)


def _silu(x):
    return x * jax.nn.sigmoid(x)


def _softplus(x):
    return jnp.maximum(x, 0.0) + jnp.log1p(jnp.exp(-jnp.abs(x)))


def _lane_col(arr, lane_iota, idx):
    return jnp.sum(jnp.where(lane_iota == idx, arr, 0.0), axis=1, keepdims=True)


def _chunk_masks(n):
    ri = lax.broadcasted_iota(jnp.int32, (n, n), 0)
    ci = lax.broadcasted_iota(jnp.int32, (n, n), 1)
    same = (ri >> CHUNK_SHIFT) == (ci >> CHUNK_SHIFT)
    return ri, ci, same, same & (ci <= ri)


def _chunk_cumsum(causal, x):
    n = x.shape[0]
    cum = _dot_sel(jnp.where(causal, 1.0, 0.0).astype(BF16), x)
    total = jnp.concatenate(
        [jnp.broadcast_to(cum[c * CHUNK + CHUNK - 1:(c + 1) * CHUNK, :], (CHUNK, x.shape[1]))
         for c in range(n // CHUNK)], axis=0)
    return cum, total


def _shift_selector(n):
    ri = lax.broadcasted_iota(jnp.int32, ((CONV_W - 1) * n, n), 0)
    ci = lax.broadcasted_iota(jnp.int32, ((CONV_W - 1) * n, n), 1)
    s = ri // n + 1
    return jnp.where(ci == ri - (s - 1) * n - s, 1.0, 0.0).astype(BF16)


def _conv_silu(xb, w_ref, carry_ref, shift_sel, bias=None):
    n, width = xb.shape
    x = xb.astype(F32)
    shifted = _dot(shift_sel, xb)
    acc = w_ref[CONV_W - 1:CONV_W, :] * x
    prev = jnp.concatenate([carry_ref[...], jnp.zeros((8, width), F32)], axis=0)
    head = jnp.zeros((8, width), F32)
    for s in range(1, CONV_W):
        w_s = w_ref[CONV_W - 1 - s:CONV_W - s, :]
        acc = acc + w_s * shifted[(s - 1) * n:s * n]
        head = head + w_s * prev[8 - s:16 - s]
    carry_ref[...] = x[n - 8:n, :]
    acc = jnp.concatenate([acc[0:8] + head, acc[8:]], axis=0)
    if bias is not None:
        acc = acc + bias
    return _silu(acc)


def _inproj_kernel(x_ref, wb_ref, ws_ref, ob_ref, os_ref):
    x = x_ref[...]
    ob_ref[...] = _dot(x, wb_ref[...]).astype(ob_ref.dtype)

    @pl.when(pl.program_id(1) == 0)
    def _():
        os_ref[...] = _dot(x, ws_ref[...])


def _inproj(xb, w_big, w_small):
    t = xb.shape[0]
    tm = min(PROJ_TM, t)
    assert t % tm == 0 and P_WIDTH % PROJ_TN == 0
    return pl.pallas_call(
        _inproj_kernel,
        grid=(t // tm, P_WIDTH // PROJ_TN),
        in_specs=[pl.BlockSpec((tm, D_MODEL), lambda i, j: (i, 0)),
                  pl.BlockSpec((D_MODEL, PROJ_TN), lambda i, j: (0, j)),
                  pl.BlockSpec((D_MODEL, LANES), lambda i, j: (0, 0))],
        out_specs=[pl.BlockSpec((tm, PROJ_TN), lambda i, j: (i, j)),
                   pl.BlockSpec((tm, LANES), lambda i, j: (i, 0))],
        out_shape=[jax.ShapeDtypeStruct((t, P_WIDTH), BF16), jax.ShapeDtypeStruct((t, LANES), F32)],
        compiler_params=_cparams(("parallel", "arbitrary")),
        name="inproj",
    )(xb, w_big, w_small)


def _mixa_kernel(q_ref, k_ref, v_ref, z_ref, sm_ref, cwq_ref, cwk_ref, cwv_ref, alog_ref, dtb_ref, gain_ref,
                 o_ref, s_ref, cq_ref, ck_ref, cv_ref, gct_ref, *, hb):
    hg = pl.program_id(1)
    n = q_ref.shape[0]

    @pl.when(pl.program_id(2) == 0)
    def _():
        s_ref[...] = jnp.zeros_like(s_ref)
        cq_ref[...] = jnp.zeros_like(cq_ref)
        ck_ref[...] = jnp.zeros_like(ck_ref)
        cv_ref[...] = jnp.zeros_like(cv_ref)

    shift_sel = _shift_selector(n)
    q = _conv_silu(q_ref[...], cwq_ref, cq_ref, shift_sel)
    k = _conv_silu(k_ref[...], cwk_ref, ck_ref, shift_sel)
    v = _conv_silu(v_ref[...], cwv_ref, cv_ref, shift_sel)

    sm = sm_ref[...]
    lane = lax.broadcasted_iota(jnp.int32, (n, LANES), 1)
    beta_all = jax.nn.sigmoid(sm)
    g_all = -jnp.exp(alog_ref[...]) * _softplus(sm + dtb_ref[...])
    _, _, _, causal = _chunk_masks(n)
    gc, gl = _chunk_cumsum(causal, g_all)
    egc, ekd, edl = jnp.exp(gc), jnp.exp(gl - gc), jnp.exp(gl)
    gct_ref[...] = gc.T

    sub = SUB
    nb = n // sub
    assert nb == 2
    ri = lax.broadcasted_iota(jnp.int32, (sub, n), 0)
    ci = lax.broadcasted_iota(jnp.int32, (sub, n), 1) & (sub - 1)
    same = (ri >> CHUNK_SHIFT) == (ci >> CHUNK_SHIFT)
    causal_c = same & (ci <= ri)
    strict_c = same & (ci < ri)
    lev = 31 - lax.clz(ri ^ ci)
    eye = jnp.where(ri == ci, 1.0, 0.0)
    zblk = jnp.zeros((sub, sub), BF16)

    def _blockdiag(m):
        return jnp.concatenate([jnp.concatenate([m[:, :sub], zblk], axis=1),
                                jnp.concatenate([zblk, m[:, sub:]], axis=1)], axis=0)

    def _side_by_side(m):
        return jnp.concatenate([m[:sub], m[sub:]], axis=1)

    heads = range(hb)
    sls = [slice(j * A_DK, (j + 1) * A_DK) for j in heads]
    e_dls, a_s, attns, rhs, wqs, kds = [], [], [], [], [], []
    for j in heads:
        h = hg * hb + j
        qh, kh, vh = q[:, sls[j]], k[:, sls[j]], v[:, sls[j]]
        qh = qh * lax.rsqrt(jnp.sum(qh * qh, axis=1, keepdims=True) + 1e-6) * (A_DK ** -0.5)
        kh = kh * lax.rsqrt(jnp.sum(kh * kh, axis=1, keepdims=True) + 1e-6)
        beta = _lane_col(beta_all, lane, S_BETA + h)
        gcol = _lane_col(gc, lane, S_ALPHA + h)
        e_gc = _lane_col(egc, lane, S_ALPHA + h)
        e_kd = _lane_col(ekd, lane, S_ALPHA + h)
        e_dls.append(_lane_col(edl, lane, S_ALPHA + h))
        grow = gct_ref[pl.ds(S_ALPHA + h, 1), :]
        gcol_c = jnp.concatenate([jnp.broadcast_to(gcol[:sub], (sub, sub)),
                                  jnp.broadcast_to(gcol[sub:], (sub, sub))], axis=1)
        dec = jnp.exp(jnp.where(causal_c, gcol_c - grow, -jnp.inf))
        kb = kh * beta
        khb = kh.astype(BF16)
        lhs = jnp.concatenate([_side_by_side(kb), _side_by_side(qh)], axis=0).astype(BF16)
        kkqk = _dot_nt(lhs, _blockdiag(_side_by_side(khb)))
        a_s.append(jnp.where(strict_c, kkqk[:sub] * dec, 0.0))
        attns.append((kkqk[sub:] * dec).astype(BF16))
        rhs.append(jnp.concatenate([vh * beta, kb * e_gc], axis=1).astype(BF16))
        wqs.append((qh * e_gc).astype(BF16))
        kds.append((kh * e_kd).astype(BF16))
    xs = [eye - jnp.where(lev == 0, a, 0.0) for a in a_s]
    for ls in range(1, CHUNK_SHIFT):
        level = lev == ls
        xbs = [x.astype(BF16) for x in xs]
        ts = [_dot(xb, _blockdiag(jnp.where(level, a, 0.0).astype(BF16))).astype(BF16) for xb, a in zip(xbs, a_s)]
        xs = [x - _dot(t, _blockdiag(xb)) for x, t, xb in zip(xs, ts, xbs)]
    uws = []
    for x, r in zip(xs, rhs):
        xb = x.astype(BF16)
        uws.append(jnp.concatenate([_dot(xb[:, b * sub:(b + 1) * sub], r[b * sub:(b + 1) * sub])
                                    for b in range(nb)], axis=0))
    ws = [uw[:, A_DV:].astype(BF16) for uw in uws]
    ss = [s_ref[j] for j in heads]
    v_new = [[] for _ in heads]
    o_inter = [[] for _ in heads]
    for c in range(n // CHUNK):
        r = slice(c * CHUNK, (c + 1) * CHUNK)
        res = [_dot(jnp.concatenate([ws[j][r], wqs[j][r]], axis=0), ss[j].astype(BF16)) for j in heads]
        for j in heads:
            o_inter[j].append(res[j][CHUNK:])
            v_new[j].append((uws[j][r, :A_DV] - res[j][:CHUNK]).astype(BF16))
        ss = [ss[j] * e_dls[j][c * CHUNK:c * CHUNK + 1, :] + _dot_tn(kds[j][r], v_new[j][c]) for j in heads]
    for j in heads:
        s_ref[j] = ss[j]
        o_intra = _dot(attns[j], _blockdiag(_side_by_side(jnp.concatenate(v_new[j], axis=0))))
        o = jnp.concatenate(o_inter[j], axis=0) + jnp.concatenate([o_intra[:, :A_DV], o_intra[:, A_DV:]], axis=0)
        o = o * lax.rsqrt(jnp.mean(o * o, axis=1, keepdims=True) + RMS_EPS) * gain_ref[...]
        o_ref[:, sls[j]] = (o * _silu(z_ref[:, sls[j]].astype(F32))).astype(o_ref.dtype)


def _mixer_a(p, psm, cwq, cwk, cwv, alog, dtb, gain, bsz, length):
    hb = A_HEADS_PER_STEP
    w = hb * A_DK
    nl = length // SUPER

    def col(base):
        return lambda b, h, l: (b * nl + l, base // w + h)

    vec = lambda b, h, l: (0, 0)
    cw = lambda b, h, l: (0, h)
    return pl.pallas_call(
        functools.partial(_mixa_kernel, hb=hb),
        grid=(bsz, A_HEADS // hb, nl),
        in_specs=[pl.BlockSpec((SUPER, w), col(P_QA)), pl.BlockSpec((SUPER, w), col(P_KA)),
                  pl.BlockSpec((SUPER, w), col(P_VA)), pl.BlockSpec((SUPER, w), col(P_ZA)),
                  pl.BlockSpec((SUPER, LANES), lambda b, h, l: (b * nl + l, 0)),
                  pl.BlockSpec((CONV_W, w), cw), pl.BlockSpec((CONV_W, w), cw), pl.BlockSpec((CONV_W, w), cw),
                  pl.BlockSpec((1, LANES), vec), pl.BlockSpec((1, LANES), vec), pl.BlockSpec((1, A_DV), vec)],
        out_specs=pl.BlockSpec((SUPER, w), lambda b, h, l: (b * nl + l, h)),
        out_shape=jax.ShapeDtypeStruct((bsz * length, A_HEADS * A_DV), BF16),
        scratch_shapes=[pltpu.VMEM((hb, A_DK, A_DV), F32),
                        pltpu.VMEM((8, w), F32), pltpu.VMEM((8, w), F32), pltpu.VMEM((8, w), F32),
                        pltpu.VMEM((LANES, SUPER), F32)],
        compiler_params=_cparams(("parallel", "parallel", "arbitrary")),
        name="mixer_a",
    )(p, p, p, p, psm, cwq, cwk, cwv, alog, dtb, gain)


def _mixb_kernel(q_ref, k_ref, v_ref, r_ref, sm_ref, wlr_ref, blr_ref, gain_ref, o_ref, st_ref, *, hb):
    n = q_ref.shape[0]

    @pl.when(pl.program_id(2) == 0)
    def _():
        st_ref[...] = jnp.zeros_like(st_ref)

    zl = _dot(sm_ref[...].astype(BF16), wlr_ref[...]) + blr_ref[...]
    log_a = -_softplus(-zl) * (1.0 / B_GATE_NORM)
    _, _, same, causal = _chunk_masks(n)
    bc, bl = _chunk_cumsum(causal, log_a)
    q = q_ref[...].astype(F32) * (B_DK ** -0.5)
    k = k_ref[...].astype(F32)
    qe = (q * jnp.exp(bc)).astype(BF16)
    ke = (k * jnp.exp(-bc)).astype(BF16)
    kd = (k * jnp.exp(bl - bc)).astype(BF16)
    edl = jnp.exp(bl)

    heads = range(hb)
    chunks = range(n // CHUNK)
    rows = [slice(c * CHUNK, (c + 1) * CHUNK) for c in chunks]
    sls = [slice(j * B_DK, (j + 1) * B_DK) for j in heads]
    svs = [slice(j * B_DV, (j + 1) * B_DV) for j in heads]
    vhs = [v_ref[:, svs[j]] for j in heads]
    upd = [[_dot_tn(vhs[j][r], kd[r, sls[j]]) for r in rows] for j in heads]
    attn = [jnp.where(causal, _dot_nt(qe[:, sls[j]], ke[:, sls[j]]), 0.0).astype(BF16) for j in heads]
    intra = [_dot(attn[j], vhs[j]) for j in heads]
    for j in heads:
        st = st_ref[j]
        inter = []
        for c in chunks:
            inter.append(_dot_nt(qe[rows[c], sls[j]], st.astype(BF16)))
            st = st * edl[c * CHUNK:c * CHUNK + 1, sls[j]] + upd[j][c]
        st_ref[j] = st
        o = intra[j] + jnp.concatenate(inter, axis=0)
        o = o * lax.rsqrt(jnp.mean(o * o, axis=1, keepdims=True) + RMS_EPS) * gain_ref[...]
        o_ref[:, svs[j]] = (o * _silu(r_ref[:, svs[j]].astype(F32))).astype(o_ref.dtype)


def _mixer_b(p, psm, wlr, blr, gain, bsz, length):
    hb = B_HEADS_PER_STEP
    wk, wv = hb * B_DK, hb * B_DV
    nl = length // SUPER

    def col(base, width):
        return lambda b, h, l: (b * nl + l, base // width + h)

    return pl.pallas_call(
        functools.partial(_mixb_kernel, hb=hb),
        grid=(bsz, B_HEADS // hb, nl),
        in_specs=[pl.BlockSpec((SUPER, wk), col(P_QB, wk)), pl.BlockSpec((SUPER, wk), col(P_KB, wk)),
                  pl.BlockSpec((SUPER, wv), col(P_VB, wv)), pl.BlockSpec((SUPER, wv), col(P_RB, wv)),
                  pl.BlockSpec((SUPER, LANES), lambda b, h, l: (b * nl + l, 0)),
                  pl.BlockSpec((LANES, wk), lambda b, h, l: (0, h)),
                  pl.BlockSpec((1, wk), lambda b, h, l: (0, h)),
                  pl.BlockSpec((1, B_DV), lambda b, h, l: (0, 0))],
        out_specs=pl.BlockSpec((SUPER, wv), lambda b, h, l: (b * nl + l, h)),
        out_shape=jax.ShapeDtypeStruct((bsz * length, B_HEADS * B_DV), BF16),
        scratch_shapes=[pltpu.VMEM((hb, B_DV, B_DK), F32)],
        compiler_params=_cparams(("parallel", "parallel", "arbitrary")),
        name="mixer_b",
    )(p, p, p, p, psm, wlr, blr, gain)


def _mixc_kernel(z_ref, x_ref, b_ref, c_ref, sm_ref, cwx_ref, cbx_ref, cwb_ref, cbb_ref, cwc_ref, cbc_ref,
                 alog_ref, dtb_ref, dskip_ref, gain_ref, o_ref,
                 st_ref, cx_ref, cbc_ref_, cst_ref):
    g = pl.program_id(1)
    n = x_ref.shape[0]

    @pl.when(pl.program_id(2) == 0)
    def _():
        st_ref[...] = jnp.zeros_like(st_ref)
        cx_ref[...] = jnp.zeros_like(cx_ref)
        cbc_ref_[...] = jnp.zeros_like(cbc_ref_)

    shift_sel = _shift_selector(n)
    xs = _conv_silu(x_ref[...], cwx_ref, cx_ref, shift_sel, cbx_ref[...])
    bcm = _conv_silu(jnp.concatenate([b_ref[...], c_ref[...]], axis=1),
                     jnp.concatenate([cwb_ref[...], cwc_ref[...]], axis=1), cbc_ref_, shift_sel,
                     jnp.concatenate([cbb_ref[...], cbc_ref[...]], axis=1)).astype(BF16)
    bm, cm = bcm[:, :C_DSTATE], bcm[:, C_DSTATE:]

    lane = lax.broadcasted_iota(jnp.int32, (n, LANES), 1)
    dt_all = _softplus(sm_ref[...] + dtb_ref[...])
    da = dt_all * (-jnp.exp(alog_ref[...]))
    _, _, same, causal = _chunk_masks(n)
    cs, _ = _chunk_cumsum(causal, da)
    cst_ref[...] = cs.T
    er = lax.broadcasted_iota(jnp.int32, (LANES, C_GROUP_W), 0)
    ec = lax.broadcasted_iota(jnp.int32, (LANES, C_GROUP_W), 1)
    head0 = S_DT + g * C_HEADS_PER_GROUP
    expand = jnp.where(er == head0 + ec // C_HEADDIM, 1.0, 0.0).astype(BF16)
    dt_e = _dot(dt_all.astype(BF16), expand)
    cs_e = _x_dot_sel(cs, expand)
    csl_e = jnp.concatenate(
        [jnp.broadcast_to(cs_e[c * CHUNK + CHUNK - 1:(c + 1) * CHUNK, :], (CHUNK, C_GROUP_W))
         for c in range(n // CHUNK)], axis=0)
    ecs, edl = jnp.exp(cs_e), jnp.exp(csl_e)
    xdt = xs * dt_e
    xdt_b = xdt.astype(BF16)
    xdtw = (xdt * jnp.exp(csl_e - cs_e)).astype(BF16)
    sub = SUB
    assert n == 2 * sub
    ri = lax.broadcasted_iota(jnp.int32, (sub, n), 0)
    ci = lax.broadcasted_iota(jnp.int32, (sub, n), 1) & (sub - 1)
    causal_c = ((ri >> CHUNK_SHIFT) == (ci >> CHUNK_SHIFT)) & (ci <= ri)
    zblk = jnp.zeros((sub, LANES), BF16)

    def _blockdiag(top, bottom):
        return jnp.concatenate([jnp.concatenate([top, zblk], axis=1),
                                jnp.concatenate([zblk, bottom], axis=1)], axis=0)

    scores = jnp.where(causal_c, _dot_nt(jnp.concatenate([cm[:sub], cm[sub:]], axis=1),
                                         _blockdiag(bm[:sub], bm[sub:])), 0.0)
    half = lax.broadcasted_iota(jnp.int32, (n, LANES), 1) // C_HEADDIM
    y_intra = []
    for pair in range(C_HEADS_PER_GROUP // 2):
        xp = xdt_b[:, pair * LANES:(pair + 1) * LANES]
        acc = None
        for side in range(2):
            hidx = head0 + 2 * pair + side
            ccol = _lane_col(cs, lane, hidx)
            ccol_c = jnp.concatenate([jnp.broadcast_to(ccol[:sub], (sub, sub)),
                                      jnp.broadcast_to(ccol[sub:], (sub, sub))], axis=1)
            crow = cst_ref[pl.ds(hidx, 1), :]
            dec = jnp.exp(jnp.where(causal_c, ccol_c - crow, -jnp.inf))
            xm = jnp.where(half == side, xp, jnp.zeros_like(xp))
            part = _dot((scores * dec).astype(BF16), _blockdiag(xm[:sub], xm[sub:]))
            acc = part if acc is None else acc + part
        y_intra.append(jnp.concatenate([acc[:, :LANES], acc[:, LANES:]], axis=0))
    y = jnp.concatenate(y_intra, axis=1)

    st = st_ref[...]
    rows = [slice(c * CHUNK, (c + 1) * CHUNK) for c in range(n // CHUNK)]
    upd = [_dot_tn(bm[r], xdtw[r]) for r in rows]
    y_inter = []
    for c, r in enumerate(rows):
        y_inter.append(_dot(cm[r], st.astype(BF16)) * ecs[r])
        st = st * edl[c * CHUNK:c * CHUNK + 1, :] + upd[c]
    st_ref[...] = st
    y = y + jnp.concatenate(y_inter, axis=0) + xs * dskip_ref[...]
    yz = y * _silu(z_ref[...].astype(F32))
    yz = yz * lax.rsqrt(jnp.mean(yz * yz, axis=1, keepdims=True) + RMS_EPS) * gain_ref[...]
    o_ref[...] = yz.astype(o_ref.dtype)


def _mixer_c(p, psm, cwx, cbx, cwb, cbb, cwc, cbc, alog, dtb, dskip, gain, bsz, length):
    nl = length // SUPER
    gw, ds = C_GROUP_W, C_DSTATE

    def col(base, width):
        return lambda b, g, l: (b * nl + l, base // width + g)

    grp = lambda b, g, l: (0, g)
    vec = lambda b, g, l: (0, 0)
    return pl.pallas_call(
        _mixc_kernel,
        grid=(bsz, C_GROUPS, nl),
        in_specs=[pl.BlockSpec((SUPER, gw), col(P_ZC, gw)), pl.BlockSpec((SUPER, gw), col(P_XC, gw)),
                  pl.BlockSpec((SUPER, ds), col(P_BC, ds)), pl.BlockSpec((SUPER, ds), col(P_CC, ds)),
                  pl.BlockSpec((SUPER, LANES), lambda b, g, l: (b * nl + l, 0)),
                  pl.BlockSpec((CONV_W, gw), grp), pl.BlockSpec((1, gw), grp),
                  pl.BlockSpec((CONV_W, ds), grp), pl.BlockSpec((1, ds), grp),
                  pl.BlockSpec((CONV_W, ds), grp), pl.BlockSpec((1, ds), grp),
                  pl.BlockSpec((1, LANES), vec), pl.BlockSpec((1, LANES), vec),
                  pl.BlockSpec((1, gw), grp), pl.BlockSpec((1, gw), grp)],
        out_specs=pl.BlockSpec((SUPER, gw), lambda b, g, l: (b * nl + l, g)),
        out_shape=jax.ShapeDtypeStruct((bsz * length, C_DINNER), BF16),
        scratch_shapes=[pltpu.VMEM((ds, gw), F32),
                        pltpu.VMEM((8, gw), F32), pltpu.VMEM((8, 2 * ds), F32),
                        pltpu.VMEM((LANES, SUPER), F32)],
        compiler_params=_cparams(("parallel", "parallel", "arbitrary")),
        name="mixer_c",
    )(p, p, p, p, psm, cwx, cbx, cwb, cbb, cwc, cbc, alog, dtb, dskip, gain)


def _layernorm(y, g, b):
    mu = jnp.mean(y, axis=1, keepdims=True)
    yc = y - mu
    var = jnp.mean(yc * yc, axis=1, keepdims=True)
    return yc * lax.rsqrt(var + LN_EPS) * g + b


def _merge_kernel(oa_ref, ob_ref, oc_ref, ga_ref, gb_ref, gc_ref, woa_ref, wob_ref, woc_ref, m_ref):
    merged = jax.nn.sigmoid(ga_ref[...].astype(F32)) * _dot(oa_ref[...], woa_ref[...])
    merged = merged + jax.nn.sigmoid(gb_ref[...].astype(F32)) * _dot(ob_ref[...], wob_ref[...])
    merged = merged + jax.nn.sigmoid(gc_ref[...].astype(F32)) * _dot(oc_ref[...], woc_ref[...])
    m_ref[...] = merged.astype(m_ref.dtype)


def _merge(oa, ob, oc, p, woa, wob, woc):
    t = oa.shape[0]
    tm = min(POST_TM, t)
    d = D_MODEL
    row = lambda i: (i, 0)
    wspec = lambda shape: pl.BlockSpec(shape, lambda i: (0, 0), pipeline_mode=pl.Buffered(1))
    return pl.pallas_call(
        _merge_kernel,
        grid=(t // tm,),
        in_specs=[pl.BlockSpec((tm, 1024), row), pl.BlockSpec((tm, 1024), row), pl.BlockSpec((tm, 1024), row),
                  pl.BlockSpec((tm, d), lambda i: (i, P_GATES // d)),
                  pl.BlockSpec((tm, d), lambda i: (i, P_GATES // d + 1)),
                  pl.BlockSpec((tm, d), lambda i: (i, P_GATES // d + 2)),
                  wspec((1024, d)), wspec((1024, d)), wspec((1024, d))],
        out_specs=pl.BlockSpec((tm, d), row),
        out_shape=jax.ShapeDtypeStruct((t, d), BF16),
        compiler_params=_cparams(("parallel",)),
        name="merge",
    )(oa, ob, oc, p, p, p, woa, wob, woc)


def _route(logits, carry):
    tm = logits.shape[0]
    lane = lax.broadcasted_iota(jnp.int32, (tm, LANES), 1)
    neg = -jnp.inf
    is_g = lane < N_GROUPS
    gl = jnp.where(is_g, logits, neg)
    gmax = jnp.max(gl, axis=1, keepdims=True)
    gsum = jnp.sum(jnp.where(is_g, jnp.exp(logits - gmax), 0.0), axis=1, keepdims=True)
    p_g = 1.0 / gsum
    g_sel = jnp.min(jnp.where(gl == gmax, lane, LANES), axis=1, keepdims=True)
    in_g = (lane >= N_GROUPS) & (lane < N_GROUPS + N_EXPERTS) & (((lane - N_GROUPS) >> 2) == g_sel)
    emax = jnp.max(jnp.where(in_g, logits, neg), axis=1, keepdims=True)
    ee = jnp.where(in_g, jnp.exp(logits - emax), 0.0)
    probs = jnp.where(in_g, ee / jnp.sum(ee, axis=1, keepdims=True), -1.0)
    p1 = jnp.max(probs, axis=1, keepdims=True)
    i1 = jnp.min(jnp.where(probs == p1, lane, LANES), axis=1, keepdims=True)
    probs2 = jnp.where(lane == i1, -1.0, probs)
    p2 = jnp.max(probs2, axis=1, keepdims=True)
    i2 = jnp.min(jnp.where(probs2 == p2, lane, LANES), axis=1, keepdims=True)
    psum = p1 + p2
    w1 = p_g * (p1 / psum)
    w2 = p_g * (p2 / psum)
    l1 = (i1 - N_GROUPS) & (EXPERTS_PER_GROUP - 1)
    l2 = (i2 - N_GROUPS) & (EXPERTS_PER_GROUP - 1)
    la = jnp.minimum(l1, l2)
    lb = jnp.maximum(l1, l2)
    pair = jnp.where(la == 0, lb - 1, jnp.where(la == 1, 6 - lb, 5))
    seg = g_sel * N_PAIRS + pair
    first_is_a = l1 == jnp.where(pair < 3, PAIR_A[0], jnp.where(pair < 5, PAIR_A[3], PAIR_A[5]))
    wa = jnp.where(first_is_a, w1, w2)
    wb = jnp.where(first_is_a, w2, w1)

    oh = lane == seg
    cnt = jnp.where(oh, 1.0, 0.0)
    ri = lax.broadcasted_iota(jnp.int32, (tm, tm), 0)
    ci = lax.broadcasted_iota(jnp.int32, (tm, tm), 1)
    before = _dot(jnp.where(ci < ri, 1.0, 0.0).astype(BF16), cnt.astype(BF16)) + carry
    rank = jnp.sum(jnp.where(oh, before, 0.0), axis=1, keepdims=True).astype(jnp.int32)
    total = carry + jnp.sum(cnt, axis=0, keepdims=True)
    packed = jnp.where(lane == 0, seg, jnp.where(lane == 1, rank, jnp.zeros_like(lane)))
    return packed.T[0:8, :], jnp.where(lane == 0, wa, jnp.where(lane == 1, wb, 0.0)), total


def _post_kernel(m_ref, x_ref, wout_ref, wr_ref, lng_ref, lnb_ref, x1e_ref, ridx_ref, cnt_ref, carry_ref, y_ref):
    tm = x_ref.shape[0]
    step = pl.program_id(0)

    @pl.when(step == 0)
    def _():
        carry_ref[...] = jnp.zeros_like(carry_ref)
        y_ref[...] = jnp.zeros_like(y_ref)

    cur = step % 2
    y = y_ref[1 - cur]
    y_ref[cur] = ALPHA * x_ref[...] + _dot(m_ref[...], wout_ref[...])
    x1 = _layernorm(y, lng_ref[...], lnb_ref[...])
    x1e_ref[:, :D_MODEL] = x1

    xh = x1.astype(BF16)
    xl = (x1 - xh.astype(F32)).astype(BF16)
    prod = _dot(xh, wr_ref[...]) + _dot(xl, wr_ref[...])
    logits = prod[:, :LANES] + prod[:, LANES:]
    sub = min(ROUTE_ROWS, tm)
    total = carry_ref[0:1, :]
    for r0 in range(0, tm, sub):
        rows_t, w_lanes, total = _route(logits[r0:r0 + sub], total)
        ridx_ref[:, r0:r0 + sub] = rows_t
        x1e_ref[r0:r0 + sub, D_MODEL:] = w_lanes
    total = total * jnp.where(step > 0, 1.0, 0.0)
    carry_ref[...] = jnp.broadcast_to(total, carry_ref.shape)
    cnt_ref[...] = jnp.broadcast_to(total, cnt_ref.shape)


def _post(merged, x, wout, wr, lng, lnb):
    t = x.shape[0]
    tm = min(POST_TM, t)
    d = D_MODEL
    n = t // tm
    row_in = lambda i: (jnp.minimum(i, n - 1), 0)
    row_out = lambda i: (jnp.maximum(i - 1, 0), 0)
    const = lambda i: (0, 0)
    wspec = lambda shape: pl.BlockSpec(shape, const, pipeline_mode=pl.Buffered(1))
    return pl.pallas_call(
        _post_kernel,
        grid=(n + 1,),
        in_specs=[pl.BlockSpec((tm, d), row_in), pl.BlockSpec((tm, d), row_in),
                  wspec((d, d)), wspec((d, 2 * LANES)), wspec((1, d)), wspec((1, d))],
        out_specs=[pl.BlockSpec((tm, XE_W), row_out), pl.BlockSpec((8, tm), lambda i: (0, jnp.maximum(i - 1, 0))),
                   pl.BlockSpec((8, LANES), const)],
        out_shape=[jax.ShapeDtypeStruct((t, XE_W), F32), jax.ShapeDtypeStruct((8, t), jnp.int32),
                   jax.ShapeDtypeStruct((8, LANES), F32)],
        scratch_shapes=[pltpu.VMEM((8, LANES), F32), pltpu.VMEM((2, tm, d), F32)],
        compiler_params=_cparams(("arbitrary",)),
        name="ln_router",
    )(merged, x, wout, wr, lng, lnb)


def _row_copy(src_ref, src_row, dst_ref, dst_row, sem):
    return pltpu.make_async_copy(src_ref.at[pl.ds(src_row, 1), :], dst_ref.at[pl.ds(dst_row, 1), :], sem)


def _dispatch_kernel(slot_ref, pad_lo_ref, pad_hi_ref, x_ref, xs_ref, zbuf_ref, sem):
    tm = x_ref.shape[0] * ROW_GROUP
    step = pl.program_id(0)
    base = step * tm

    @pl.when(step == 0)
    def _():
        zbuf_ref[...] = jnp.zeros_like(zbuf_ref)

        def zero_copy(lo, kk):
            return _row_copy(zbuf_ref, 0, xs_ref, lo + kk, sem)

        def bucket(sg, carry):
            lo = pad_lo_ref[sg]
            n_copies = pad_hi_ref[sg] - lo

            def start(kk, c):
                zero_copy(lo, kk).start()
                return c

            def wait(kk, c):
                zero_copy(lo, kk).wait()
                return c

            lax.fori_loop(0, n_copies, start, 0)
            lax.fori_loop(0, n_copies, wait, 0)
            return carry

        lax.fori_loop(0, N_SEG, bucket, 0)

        zr = zbuf_ref.shape[0]
        tail_lo = pad_hi_ref[N_SEG - 1]

        def tail_copy(kk):
            return pltpu.make_async_copy(zbuf_ref, xs_ref.at[pl.ds(pl.multiple_of(tail_lo + kk * zr, zr), zr), :], sem)

        def tail_start(kk, c):
            tail_copy(kk).start()
            return c

        def tail_wait(kk, c):
            tail_copy(kk).wait()
            return c

        n_tail = (xs_ref.shape[0] - tail_lo) // zr
        lax.fori_loop(0, n_tail, tail_start, 0)
        lax.fori_loop(0, n_tail, tail_wait, 0)

    def start_rows(g, carry):
        r0 = pl.multiple_of(g * ROW_GROUP, ROW_GROUP)
        for u in range(ROW_GROUP):
            pltpu.make_async_copy(x_ref.at[g, pl.ds(u, 1), :],
                                  xs_ref.at[pl.ds(slot_ref[base + r0 + u], 1), :], sem).start()
        return carry

    lax.fori_loop(0, tm // ROW_GROUP, start_rows, 0)
    pltpu.make_async_copy(xs_ref.at[pl.ds(0, tm), :], xs_ref.at[pl.ds(0, tm), :], sem).wait()


def _dispatch(slot, pad_lo, pad_hi, x1e, n_slots):
    t = x1e.shape[0]
    tm = min(DISPATCH_TM, t)
    return pl.pallas_call(
        _dispatch_kernel,
        grid_spec=pltpu.PrefetchScalarGridSpec(
            num_scalar_prefetch=3, grid=(t // tm,),
            in_specs=[pl.BlockSpec((tm // ROW_GROUP, ROW_GROUP, XE_W), lambda i, s, lo, hi: (i, 0, 0))],
            out_specs=pl.BlockSpec(memory_space=pl.ANY),
            scratch_shapes=[pltpu.VMEM((ZERO_ROWS, XE_W), F32), pltpu.SemaphoreType.DMA(())]),
        out_shape=jax.ShapeDtypeStruct((n_slots, XE_W), F32),
        compiler_params=_cparams(("arbitrary",)),
        name="moe_dispatch",
    )(slot, pad_lo, pad_hi, x1e.reshape(t // ROW_GROUP, ROW_GROUP, XE_W))


def _expert_kernel(blk_ref, ea_ref, eb_ref, valid_ref, nused_ref, x_ref, wga_ref, wua_ref, wda_ref,
                   wgb_ref, wub_ref, wdb_ref, y_ref):
    del blk_ref, ea_ref, eb_ref
    tm = x_ref.shape[0]
    half = tm // 2
    step = pl.program_id(0)
    used = step < nused_ref[0]
    full = valid_ref[step] > half

    def run(rows):
        x = x_ref[0:rows, :]
        xb = x[:, :D_MODEL].astype(BF16)
        ha = (_silu(_dot(xb, wga_ref[0])) * _dot(xb, wua_ref[0])).astype(BF16)
        hb = (_silu(_dot(xb, wgb_ref[0])) * _dot(xb, wub_ref[0])).astype(BF16)
        wa = x[:, D_MODEL:D_MODEL + 1]
        wb = x[:, D_MODEL + 1:D_MODEL + 2]
        y_ref[0:rows, :] = wa * _dot(ha, wda_ref[0]) + wb * _dot(hb, wdb_ref[0])

    @pl.when(used & full)
    def _():
        run(tm)

    @pl.when(used & jnp.logical_not(full))
    def _():
        run(half)
        y_ref[half:tm, :] = jnp.zeros((tm - half, D_MODEL), F32)

    @pl.when(jnp.logical_not(used))
    def _():
        y_ref[...] = jnp.zeros_like(y_ref)


def _experts(tile_blk, tile_ea, tile_eb, tile_valid, n_used, xs, wg, wu, wd):
    tm = MOE_TM
    n_tiles = tile_blk.shape[0]
    d, de = D_MODEL, D_EXPERT

    def wspec(shape, which):
        return pl.BlockSpec(shape, lambda i, tb, ea, eb, nv, nu: ((ea, eb)[which][i], 0, 0))

    return pl.pallas_call(
        _expert_kernel,
        grid_spec=pltpu.PrefetchScalarGridSpec(
            num_scalar_prefetch=5, grid=(n_tiles,),
            in_specs=[pl.BlockSpec((tm, XE_W), lambda i, tb, ea, eb, nv, nu: (tb[i], 0)),
                      wspec((1, d, de), 0), wspec((1, d, de), 0), wspec((1, de, d), 0),
                      wspec((1, d, de), 1), wspec((1, d, de), 1), wspec((1, de, d), 1)],
            out_specs=pl.BlockSpec((tm, d), lambda i, tb, ea, eb, nv, nu: (i, 0))),
        out_shape=jax.ShapeDtypeStruct((n_tiles * tm, d), F32),
        compiler_params=_cparams(("arbitrary",), EXPERT_VMEM_LIMIT_BYTES),
        name="moe_experts",
    )(tile_blk, tile_ea, tile_eb, tile_valid, n_used, xs, wg, wu, wd, wg, wu, wd)


def _combine_kernel(slot_ref, ys_ref, x1_ref, lng_ref, lnb_ref, x2_ref, x2b_ref, ybuf_ref, sem):
    tm = x1_ref.shape[0]
    step = pl.program_id(0)

    def gather(s):
        buf = s % 2

        def start_rows(g, carry):
            r0 = pl.multiple_of(g * ROW_GROUP, ROW_GROUP)
            for u in range(ROW_GROUP):
                pltpu.make_async_copy(ys_ref.at[pl.ds(slot_ref[s * tm + r0 + u], 1), :],
                                      ybuf_ref.at[buf, g, pl.ds(u, 1), :], sem.at[buf]).start()
            return carry

        lax.fori_loop(0, tm // ROW_GROUP, start_rows, 0)

    @pl.when(step == 0)
    def _():
        gather(step)

    @pl.when(step + 1 < pl.num_programs(0))
    def _():
        gather(step + 1)

    buf = step % 2
    pltpu.make_async_copy(ys_ref.at[pl.ds(0, tm), :], ys_ref.at[pl.ds(0, tm), :], sem.at[buf]).wait()
    y = ybuf_ref[buf].reshape(tm, D_MODEL)
    x2 = _layernorm(ALPHA * x1_ref[...] + y, lng_ref[...], lnb_ref[...])
    x2_ref[...] = x2
    x2b_ref[...] = x2.astype(BF16)


def _combine(slot, ys, x1e, lng, lnb):
    t = x1e.shape[0]
    tm = min(COMBINE_TM, t)
    d = D_MODEL
    return pl.pallas_call(
        _combine_kernel,
        grid_spec=pltpu.PrefetchScalarGridSpec(
            num_scalar_prefetch=1, grid=(t // tm,),
            in_specs=[pl.BlockSpec(memory_space=pl.ANY),
                      pl.BlockSpec((tm, d), lambda i, s: (i, 0)),
                      pl.BlockSpec((1, d), lambda i, s: (0, 0)), pl.BlockSpec((1, d), lambda i, s: (0, 0))],
            out_specs=[pl.BlockSpec((tm, d), lambda i, s: (i, 0)), pl.BlockSpec((tm, d), lambda i, s: (i, 0))],
            scratch_shapes=[pltpu.VMEM((2, tm // ROW_GROUP, ROW_GROUP, d), F32), pltpu.SemaphoreType.DMA((2,))]),
        out_shape=[jax.ShapeDtypeStruct((t, d), F32), jax.ShapeDtypeStruct((t, d), BF16)],
        compiler_params=_cparams(("arbitrary",)),
        name="moe_combine_ln",
    )(slot, ys, x1e, lng, lnb)


def _routing_tables(ridx, cnt, t):
    tm = MOE_TM
    n_tiles = t // tm + N_SEG
    counts = cnt[0, :N_SEG].astype(jnp.int32)
    padded = ((counts + tm - 1) // tm) * tm
    ends = jnp.cumsum(padded)
    offs = ends - padded
    slot = offs[ridx[0]] + ridx[1]
    n_used = ends[-1] // tm
    tile_blk = jnp.minimum(jnp.arange(n_tiles, dtype=jnp.int32), n_used - 1)
    tile_seg = jnp.minimum(jnp.sum(tile_blk[:, None] * tm >= ends[None, :], axis=1), N_SEG - 1)
    tile_valid = jnp.clip(offs[tile_seg] + counts[tile_seg] - tile_blk * tm, 0, tm)
    first = (tile_seg // N_PAIRS) * EXPERTS_PER_GROUP
    tile_ea = first + jnp.asarray(PAIR_A, jnp.int32)[tile_seg % N_PAIRS]
    tile_eb = first + jnp.asarray(PAIR_B, jnp.int32)[tile_seg % N_PAIRS]
    i32 = lambda a: a.astype(jnp.int32)
    return (i32(slot), i32(offs + counts), i32(ends), i32(tile_blk), i32(tile_ea), i32(tile_eb), i32(tile_valid),
            i32(n_used.reshape(1)), n_tiles * tm)


def _lanes(vec, start):
    return jnp.zeros((1, LANES), F32).at[0, start:start + vec.shape[0]].set(vec.astype(F32))


def _layer_params(w_in, conv_a_w, a_log_a, dt_bias_a, norm_a, w_lr_b, b_lr_b, norm_b, conv_c_w, conv_c_b,
                  a_log_c, dt_bias_c, d_c, norm_c, w_oa, w_ob, w_oc, w_out, ln1_g, ln1_b, w_rg, w_re,
                  w_gate_e, w_up_e, w_down_e, ln2_g, ln2_b):
    def cols(a, n):
        return w_in[:, a:a + n]

    w_big = jnp.concatenate([
        cols(_GATES, 3 * D_MODEL), cols(_QA, 1024), cols(_KA, 1024), cols(_VA, 1024), cols(_ZA, 1024),
        cols(_QB, 512), cols(_KB, 512), cols(_VB, 1024), cols(_RB, 1024),
        cols(_ZC, 1024), cols(_XBC, 1536)], axis=1).astype(BF16)
    w_small = jnp.concatenate([cols(_BETA, 8), cols(_ALPHA_A, 8), cols(_LRB, 16), cols(_DTC, 16),
                               jnp.zeros((D_MODEL, LANES - 48), F32)], axis=1).astype(BF16)
    w_r = jnp.concatenate([w_rg, w_re, jnp.zeros((D_MODEL, LANES - N_GROUPS - N_EXPERTS), F32)], axis=1)
    w_rh = w_r.astype(BF16)
    w_rl = (w_r - w_rh.astype(F32)).astype(BF16)
    wlr = jnp.zeros((LANES, B_HEADS * B_DK), F32).at[S_LRB:S_LRB + B_GATE_RANK].set(w_lr_b).astype(BF16)
    row = lambda v: v.reshape(1, -1).astype(F32)
    return dict(
        w_big=w_big, w_small=w_small,
        cwq=conv_a_w[:, :1024], cwk=conv_a_w[:, 1024:2048], cwv=conv_a_w[:, 2048:],
        alog_a=_lanes(a_log_a, S_ALPHA), dtb_a=_lanes(dt_bias_a, S_ALPHA), gain_a=row(norm_a),
        wlr=wlr, blr=row(b_lr_b), gain_b=row(norm_b),
        cwx=conv_c_w[:, :C_DINNER], cbx=row(conv_c_b[:C_DINNER]),
        cwb=conv_c_w[:, C_DINNER:C_DINNER + 256], cbb=row(conv_c_b[C_DINNER:C_DINNER + 256]),
        cwc=conv_c_w[:, C_DINNER + 256:], cbc=row(conv_c_b[C_DINNER + 256:]),
        alog_c=_lanes(a_log_c, S_DT), dtb_c=_lanes(dt_bias_c, S_DT),
        dskip=row(jnp.repeat(d_c, C_HEADDIM)), gain_c=row(norm_c),
        woa=w_oa.astype(BF16), wob=w_ob.astype(BF16), woc=w_oc.astype(BF16), wout=w_out.astype(BF16),
        w_r=jnp.concatenate([w_rh, w_rl], axis=1), ln1_g=row(ln1_g), ln1_b=row(ln1_b),
        wg=w_gate_e.astype(BF16), wu=w_up_e.astype(BF16), wd=w_down_e.astype(BF16),
        ln2_g=row(ln2_g), ln2_b=row(ln2_b))


def _layer(x, xb, pr, bsz, length):
    t = bsz * length
    p, psm = _inproj(xb, pr["w_big"], pr["w_small"])
    oa = _mixer_a(p, psm, pr["cwq"], pr["cwk"], pr["cwv"], pr["alog_a"], pr["dtb_a"], pr["gain_a"], bsz, length)
    ob = _mixer_b(p, psm, pr["wlr"], pr["blr"], pr["gain_b"], bsz, length)
    oc = _mixer_c(p, psm, pr["cwx"], pr["cbx"], pr["cwb"], pr["cbb"], pr["cwc"], pr["cbc"],
                  pr["alog_c"], pr["dtb_c"], pr["dskip"], pr["gain_c"], bsz, length)
    merged = _merge(oa, ob, oc, p, pr["woa"], pr["wob"], pr["woc"])
    x1e, ridx, cnt = _post(merged, x, pr["wout"], pr["w_r"], pr["ln1_g"], pr["ln1_b"])
    slot, pad_lo, pad_hi, tile_blk, tile_ea, tile_eb, tile_valid, n_used, n_slots = _routing_tables(ridx, cnt, t)
    xs = _dispatch(slot, pad_lo, pad_hi, x1e, n_slots)
    ys = _experts(tile_blk, tile_ea, tile_eb, tile_valid, n_used, xs, pr["wg"], pr["wu"], pr["wd"])
    return _combine(slot, ys, x1e, pr["ln2_g"], pr["ln2_b"])


def kernel(x, w_in, conv_a_w, a_log_a, dt_bias_a, norm_a, w_lr_b, b_lr_b, norm_b, conv_c_w, conv_c_b, a_log_c,
           dt_bias_c, d_c, norm_c, w_oa, w_ob, w_oc, w_out, ln1_g, ln1_b, w_rg, w_re, w_gate_e, w_up_e,
           w_down_e, ln2_g, ln2_b):
    bsz, length, d = x.shape
    assert d == D_MODEL and length % SUPER == 0
    params = (w_in, conv_a_w, a_log_a, dt_bias_a, norm_a, w_lr_b, b_lr_b, norm_b, conv_c_w, conv_c_b, a_log_c,
              dt_bias_c, d_c, norm_c, w_oa, w_ob, w_oc, w_out, ln1_g, ln1_b, w_rg, w_re, w_gate_e, w_up_e,
              w_down_e, ln2_g, ln2_b)
    xf = x.reshape(bsz * length, d)
    xb = xf.astype(BF16)
    for layer in range(params[0].shape[0]):
        pr = _layer_params(*(p[layer] for p in params))
        xf, xb = _layer(xf, xb, pr, bsz, length)
    return xf.reshape(bsz, length, d)
```

```python
import functools

import jax
import jax.numpy as jnp
import numpy as np
from jax import lax
from jax.experimental import pallas as pl
from jax.experimental.pallas import tpu as pltpu

F32 = jnp.float32
BF16 = jnp.bfloat16

D_MODEL = 2048
DEPTH = 2
CHUNK = 64
CHUNK_SHIFT = 6
CONV_W = 4
A_HEADS, A_DK, A_DV = 8, 128, 128
B_HEADS, B_DK, B_DV = 4, 128, 256
B_GATE_RANK = 16
B_GATE_NORM = 16.0
C_HEADS, C_HEADDIM, C_GROUPS, C_DSTATE = 16, 64, 2, 128
C_DINNER = C_HEADS * C_HEADDIM
C_GROUP_W = C_DINNER // C_GROUPS
C_HEADS_PER_GROUP = C_HEADS // C_GROUPS
N_GROUPS, EXPERTS_PER_GROUP = 4, 4
N_EXPERTS = N_GROUPS * EXPERTS_PER_GROUP
D_EXPERT = 768
PAIR_A = (0, 0, 0, 1, 1, 3)
PAIR_B = (1, 2, 3, 3, 2, 2)
N_PAIRS = len(PAIR_A)
N_SEG = N_GROUPS * N_PAIRS
LN_EPS = 1e-5
RMS_EPS = 1e-6
ALPHA = (2 * DEPTH) ** 0.25

_QA, _KA, _VA, _ZA, _BETA, _ALPHA_A = 0, 1024, 2048, 3072, 4096, 4104
_QB, _KB, _VB, _RB, _LRB = 4112, 4624, 5136, 6160, 7184
_ZC, _XBC, _DTC, _GATES = 7200, 8224, 9760, 9776
P_GATES, P_QA, P_KA, P_VA, P_ZA = 0, 6144, 7168, 8192, 9216
P_QB, P_KB, P_VB, P_RB = 10240, 10752, 11264, 12288
P_ZC, P_XC, P_BC, P_CC = 13312, 14336, 15360, 15616
P_WIDTH = 15872
S_BETA, S_ALPHA, S_LRB, S_DT = 0, 8, 16, 32
LANES = 128
XE_W = D_MODEL + LANES
ZERO_ROWS = 64

VMEM_LIMIT_BYTES = 56 * 1024 * 1024
EXPERT_VMEM_LIMIT_BYTES = 62 * 1024 * 1024

SUPER = 256
SUB = 128
A_HEADS_PER_STEP = 8
B_HEADS_PER_STEP = 4
PROJ_TM, PROJ_TN = 2048, 512
POST_TM = 512
ROUTE_ROWS = 128
MOE_TM = 512
DISPATCH_TM = 1024
COMBINE_TM = 512
ROW_GROUP = 8


def _cparams(sem, vmem_limit_bytes=VMEM_LIMIT_BYTES):
    return pltpu.CompilerParams(dimension_semantics=sem, vmem_limit_bytes=vmem_limit_bytes)


def _dot(a, b):
    return jnp.dot(a, b, preferred_element_type=F32)


def _dot_nt(a, b):
    return lax.dot_general(a, b, (((1,), (1,)), ((), ())), preferred_element_type=F32)


def _dot_tn(a, b):
    return lax.dot_general(a, b, (((0,), (0,)), ((), ())), preferred_element_type=F32)


def _split3(x):
    hi = x.astype(BF16)
    r = x - hi.astype(F32)
    mid = r.astype(BF16)
    lo = (r - mid.astype(F32)).astype(BF16)
    return hi, mid, lo


def _dot_sel(sel_bf16, x):
    hi, mid, lo = _split3(x)
    return _dot(sel_bf16, hi) + _dot(sel_bf16, mid) + _dot(sel_bf16, lo)


def _x_dot_sel(x, sel_bf16):
    hi, mid, lo = _split3(x)
    return _dot(hi, sel_bf16) + _dot(mid, sel_bf16) + _dot(lo, sel_bf16)


def _silu(x):
    return x * jax.nn.sigmoid(x)


def _softplus(x):
    return jnp.maximum(x, 0.0) + jnp.log1p(jnp.exp(-jnp.abs(x)))


def _lane_col(arr, lane_iota, idx):
    return jnp.sum(jnp.where(lane_iota == idx, arr, 0.0), axis=1, keepdims=True)


def _chunk_masks(n):
    ri = lax.broadcasted_iota(jnp.int32, (n, n), 0)
    ci = lax.broadcasted_iota(jnp.int32, (n, n), 1)
    same = (ri >> CHUNK_SHIFT) == (ci >> CHUNK_SHIFT)
    return ri, ci, same, same & (ci <= ri)


def _chunk_cumsum(causal, x):
    n = x.shape[0]
    cum = _dot_sel(jnp.where(causal, 1.0, 0.0).astype(BF16), x)
    total = jnp.concatenate(
        [jnp.broadcast_to(cum[c * CHUNK + CHUNK - 1:(c + 1) * CHUNK, :], (CHUNK, x.shape[1]))
         for c in range(n // CHUNK)], axis=0)
    return cum, total


def _shift_selector(n):
    ri = lax.broadcasted_iota(jnp.int32, ((CONV_W - 1) * n, n), 0)
    ci = lax.broadcasted_iota(jnp.int32, ((CONV_W - 1) * n, n), 1)
    s = ri // n + 1
    return jnp.where(ci == ri - (s - 1) * n - s, 1.0, 0.0).astype(BF16)


def _conv_silu(xb, w_ref, carry_ref, shift_sel, bias=None):
    n, width = xb.shape
    x = xb.astype(F32)
    shifted = _dot(shift_sel, xb)
    acc = w_ref[CONV_W - 1:CONV_W, :] * x
    prev = jnp.concatenate([carry_ref[...], jnp.zeros((8, width), F32)], axis=0)
    head = jnp.zeros((8, width), F32)
    for s in range(1, CONV_W):
        w_s = w_ref[CONV_W - 1 - s:CONV_W - s, :]
        acc = acc + w_s * shifted[(s - 1) * n:s * n]
        head = head + w_s * prev[8 - s:16 - s]
    carry_ref[...] = x[n - 8:n, :]
    acc = jnp.concatenate([acc[0:8] + head, acc[8:]], axis=0)
    if bias is not None:
        acc = acc + bias
    return _silu(acc)


def _inproj_kernel(x_ref, wb_ref, ws_ref, ob_ref, os_ref):
    x = x_ref[...]
    ob_ref[...] = _dot(x, wb_ref[...]).astype(ob_ref.dtype)

    @pl.when(pl.program_id(1) == 0)
    def _():
        os_ref[...] = _dot(x, ws_ref[...])


def _inproj(xb, w_big, w_small):
    t = xb.shape[0]
    tm = min(PROJ_TM, t)
    assert t % tm == 0 and P_WIDTH % PROJ_TN == 0
    return pl.pallas_call(
        _inproj_kernel,
        grid=(t // tm, P_WIDTH // PROJ_TN),
        in_specs=[pl.BlockSpec((tm, D_MODEL), lambda i, j: (i, 0)),
                  pl.BlockSpec((D_MODEL, PROJ_TN), lambda i, j: (0, j)),
                  pl.BlockSpec((D_MODEL, LANES), lambda i, j: (0, 0))],
        out_specs=[pl.BlockSpec((tm, PROJ_TN), lambda i, j: (i, j)),
                   pl.BlockSpec((tm, LANES), lambda i, j: (i, 0))],
        out_shape=[jax.ShapeDtypeStruct((t, P_WIDTH), BF16), jax.ShapeDtypeStruct((t, LANES), F32)],
        compiler_params=_cparams(("parallel", "arbitrary")),
        name="inproj",
    )(xb, w_big, w_small)


def _mixa_kernel(q_ref, k_ref, v_ref, z_ref, sm_ref, cwq_ref, cwk_ref, cwv_ref, alog_ref, dtb_ref, gain_ref,
                 o_ref, s_ref, cq_ref, ck_ref, cv_ref, gct_ref, *, hb):
    hg = pl.program_id(1)
    n = q_ref.shape[0]

    @pl.when(pl.program_id(2) == 0)
    def _():
        s_ref[...] = jnp.zeros_like(s_ref)
        cq_ref[...] = jnp.zeros_like(cq_ref)
        ck_ref[...] = jnp.zeros_like(ck_ref)
        cv_ref[...] = jnp.zeros_like(cv_ref)

    shift_sel = _shift_selector(n)
    q = _conv_silu(q_ref[...], cwq_ref, cq_ref, shift_sel)
    k = _conv_silu(k_ref[...], cwk_ref, ck_ref, shift_sel)
    v = _conv_silu(v_ref[...], cwv_ref, cv_ref, shift_sel)

    sm = sm_ref[...]
    lane = lax.broadcasted_iota(jnp.int32, (n, LANES), 1)
    beta_all = jax.nn.sigmoid(sm)
    g_all = -jnp.exp(alog_ref[...]) * _softplus(sm + dtb_ref[...])
    _, _, _, causal = _chunk_masks(n)
    gc, gl = _chunk_cumsum(causal, g_all)
    egc, ekd, edl = jnp.exp(gc), jnp.exp(gl - gc), jnp.exp(gl)
    gct_ref[...] = gc.T

    sub = SUB
    nb = n // sub
    assert nb == 2
    ri = lax.broadcasted_iota(jnp.int32, (sub, n), 0)
    ci = lax.broadcasted_iota(jnp.int32, (sub, n), 1) & (sub - 1)
    same = (ri >> CHUNK_SHIFT) == (ci >> CHUNK_SHIFT)
    causal_c = same & (ci <= ri)
    strict_c = same & (ci < ri)
    lev = 31 - lax.clz(ri ^ ci)
    eye = jnp.where(ri == ci, 1.0, 0.0)
    zblk = jnp.zeros((sub, sub), BF16)

    def _blockdiag(m):
        return jnp.concatenate([jnp.concatenate([m[:, :sub], zblk], axis=1),
                                jnp.concatenate([zblk, m[:, sub:]], axis=1)], axis=0)

    def _side_by_side(m):
        return jnp.concatenate([m[:sub], m[sub:]], axis=1)

    heads = range(hb)
    sls = [slice(j * A_DK, (j + 1) * A_DK) for j in heads]
    e_dls, a_s, attns, rhs, wqs, kds = [], [], [], [], [], []
    for j in heads:
        h = hg * hb + j
        qh, kh, vh = q[:, sls[j]], k[:, sls[j]], v[:, sls[j]]
        qh = qh * lax.rsqrt(jnp.sum(qh * qh, axis=1, keepdims=True) + 1e-6) * (A_DK ** -0.5)
        kh = kh * lax.rsqrt(jnp.sum(kh * kh, axis=1, keepdims=True) + 1e-6)
        beta = _lane_col(beta_all, lane, S_BETA + h)
        gcol = _lane_col(gc, lane, S_ALPHA + h)
        e_gc = _lane_col(egc, lane, S_ALPHA + h)
        e_kd = _lane_col(ekd, lane, S_ALPHA + h)
        e_dls.append(_lane_col(edl, lane, S_ALPHA + h))
        grow = gct_ref[pl.ds(S_ALPHA + h, 1), :]
        gcol_c = jnp.concatenate([jnp.broadcast_to(gcol[:sub], (sub, sub)),
                                  jnp.broadcast_to(gcol[sub:], (sub, sub))], axis=1)
        dec = jnp.exp(jnp.where(causal_c, gcol_c - grow, -jnp.inf))
        kb = kh * beta
        khb = kh.astype(BF16)
        lhs = jnp.concatenate([_side_by_side(kb), _side_by_side(qh)], axis=0).astype(BF16)
        kkqk = _dot_nt(lhs, _blockdiag(_side_by_side(khb)))
        a_s.append(jnp.where(strict_c, kkqk[:sub] * dec, 0.0))
        attns.append((kkqk[sub:] * dec).astype(BF16))
        rhs.append(jnp.concatenate([vh * beta, kb * e_gc], axis=1).astype(BF16))
        wqs.append((qh * e_gc).astype(BF16))
        kds.append((kh * e_kd).astype(BF16))
    xs = [eye - jnp.where(lev == 0, a, 0.0) for a in a_s]
    for ls in range(1, CHUNK_SHIFT):
        level = lev == ls
        xbs = [x.astype(BF16) for x in xs]
        ts = [_dot(xb, _blockdiag(jnp.where(level, a, 0.0).astype(BF16))).astype(BF16) for xb, a in zip(xbs, a_s)]
        xs = [x - _dot(t, _blockdiag(xb)) for x, t, xb in zip(xs, ts, xbs)]
    uws = []
    for x, r in zip(xs, rhs):
        xb = x.astype(BF16)
        uws.append(jnp.concatenate([_dot(xb[:, b * sub:(b + 1) * sub], r[b * sub:(b + 1) * sub])
                                    for b in range(nb)], axis=0))
    ws = [uw[:, A_DV:].astype(BF16) for uw in uws]
    ss = [s_ref[j] for j in heads]
    v_new = [[] for _ in heads]
    o_inter = [[] for _ in heads]
    for c in range(n // CHUNK):
        r = slice(c * CHUNK, (c + 1) * CHUNK)
        res = [_dot(jnp.concatenate([ws[j][r], wqs[j][r]], axis=0), ss[j].astype(BF16)) for j in heads]
        for j in heads:
            o_inter[j].append(res[j][CHUNK:])
            v_new[j].append((uws[j][r, :A_DV] - res[j][:CHUNK]).astype(BF16))
        ss = [ss[j] * e_dls[j][c * CHUNK:c * CHUNK + 1, :] + _dot_tn(kds[j][r], v_new[j][c]) for j in heads]
    for j in heads:
        s_ref[j] = ss[j]
        o_intra = _dot(attns[j], _blockdiag(_side_by_side(jnp.concatenate(v_new[j], axis=0))))
        o = jnp.concatenate(o_inter[j], axis=0) + jnp.concatenate([o_intra[:, :A_DV], o_intra[:, A_DV:]], axis=0)
        o = o * lax.rsqrt(jnp.mean(o * o, axis=1, keepdims=True) + RMS_EPS) * gain_ref[...]
        o_ref[:, sls[j]] = (o * _silu(z_ref[:, sls[j]].astype(F32))).astype(o_ref.dtype)


def _mixer_a(p, psm, cwq, cwk, cwv, alog, dtb, gain, bsz, length):
    hb = A_HEADS_PER_STEP
    w = hb * A_DK
    nl = length // SUPER

    def col(base):
        return lambda b, h, l: (b * nl + l, base // w + h)

    vec = lambda b, h, l: (0, 0)
    cw = lambda b, h, l: (0, h)
    return pl.pallas_call(
        functools.partial(_mixa_kernel, hb=hb),
        grid=(bsz, A_HEADS // hb, nl),
        in_specs=[pl.BlockSpec((SUPER, w), col(P_QA)), pl.BlockSpec((SUPER, w), col(P_KA)),
                  pl.BlockSpec((SUPER, w), col(P_VA)), pl.BlockSpec((SUPER, w), col(P_ZA)),
                  pl.BlockSpec((SUPER, LANES), lambda b, h, l: (b * nl + l, 0)),
                  pl.BlockSpec((CONV_W, w), cw), pl.BlockSpec((CONV_W, w), cw), pl.BlockSpec((CONV_W, w), cw),
                  pl.BlockSpec((1, LANES), vec), pl.BlockSpec((1, LANES), vec), pl.BlockSpec((1, A_DV), vec)],
        out_specs=pl.BlockSpec((SUPER, w), lambda b, h, l: (b * nl + l, h)),
        out_shape=jax.ShapeDtypeStruct((bsz * length, A_HEADS * A_DV), BF16),
        scratch_shapes=[pltpu.VMEM((hb, A_DK, A_DV), F32),
                        pltpu.VMEM((8, w), F32), pltpu.VMEM((8, w), F32), pltpu.VMEM((8, w), F32),
                        pltpu.VMEM((LANES, SUPER), F32)],
        compiler_params=_cparams(("parallel", "parallel", "arbitrary")),
        name="mixer_a",
    )(p, p, p, p, psm, cwq, cwk, cwv, alog, dtb, gain)


def _mixb_kernel(q_ref, k_ref, v_ref, r_ref, sm_ref, wlr_ref, blr_ref, gain_ref, o_ref, st_ref, *, hb):
    n = q_ref.shape[0]

    @pl.when(pl.program_id(2) == 0)
    def _():
        st_ref[...] = jnp.zeros_like(st_ref)

    zl = _dot(sm_ref[...].astype(BF16), wlr_ref[...]) + blr_ref[...]
    log_a = -_softplus(-zl) * (1.0 / B_GATE_NORM)
    _, _, same, causal = _chunk_masks(n)
    bc, bl = _chunk_cumsum(causal, log_a)
    q = q_ref[...].astype(F32) * (B_DK ** -0.5)
    k = k_ref[...].astype(F32)
    qe = (q * jnp.exp(bc)).astype(BF16)
    ke = (k * jnp.exp(-bc)).astype(BF16)
    kd = (k * jnp.exp(bl - bc)).astype(BF16)
    edl = jnp.exp(bl)

    heads = range(hb)
    chunks = range(n // CHUNK)
    rows = [slice(c * CHUNK, (c + 1) * CHUNK) for c in chunks]
    sls = [slice(j * B_DK, (j + 1) * B_DK) for j in heads]
    svs = [slice(j * B_DV, (j + 1) * B_DV) for j in heads]
    vhs = [v_ref[:, svs[j]] for j in heads]
    upd = [[_dot_tn(vhs[j][r], kd[r, sls[j]]) for r in rows] for j in heads]
    attn = [jnp.where(causal, _dot_nt(qe[:, sls[j]], ke[:, sls[j]]), 0.0).astype(BF16) for j in heads]
    intra = [_dot(attn[j], vhs[j]) for j in heads]
    for j in heads:
        st = st_ref[j]
        inter = []
        for c in chunks:
            inter.append(_dot_nt(qe[rows[c], sls[j]], st.astype(BF16)))
            st = st * edl[c * CHUNK:c * CHUNK + 1, sls[j]] + upd[j][c]
        st_ref[j] = st
        o = intra[j] + jnp.concatenate(inter, axis=0)
        o = o * lax.rsqrt(jnp.mean(o * o, axis=1, keepdims=True) + RMS_EPS) * gain_ref[...]
        o_ref[:, svs[j]] = (o * _silu(r_ref[:, svs[j]].astype(F32))).astype(o_ref.dtype)


def _mixer_b(p, psm, wlr, blr, gain, bsz, length):
    hb = B_HEADS_PER_STEP
    wk, wv = hb * B_DK, hb * B_DV
    nl = length // SUPER

    def col(base, width):
        return lambda b, h, l: (b * nl + l, base // width + h)

    return pl.pallas_call(
        functools.partial(_mixb_kernel, hb=hb),
        grid=(bsz, B_HEADS // hb, nl),
        in_specs=[pl.BlockSpec((SUPER, wk), col(P_QB, wk)), pl.BlockSpec((SUPER, wk), col(P_KB, wk)),
                  pl.BlockSpec((SUPER, wv), col(P_VB, wv)), pl.BlockSpec((SUPER, wv), col(P_RB, wv)),
                  pl.BlockSpec((SUPER, LANES), lambda b, h, l: (b * nl + l, 0)),
                  pl.BlockSpec((LANES, wk), lambda b, h, l: (0, h)),
                  pl.BlockSpec((1, wk), lambda b, h, l: (0, h)),
                  pl.BlockSpec((1, B_DV), lambda b, h, l: (0, 0))],
        out_specs=pl.BlockSpec((SUPER, wv), lambda b, h, l: (b * nl + l, h)),
        out_shape=jax.ShapeDtypeStruct((bsz * length, B_HEADS * B_DV), BF16),
        scratch_shapes=[pltpu.VMEM((hb, B_DV, B_DK), F32)],
        compiler_params=_cparams(("parallel", "parallel", "arbitrary")),
        name="mixer_b",
    )(p, p, p, p, psm, wlr, blr, gain)


def _mixc_kernel(z_ref, x_ref, b_ref, c_ref, sm_ref, cwx_ref, cbx_ref, cwb_ref, cbb_ref, cwc_ref, cbc_ref,
                 alog_ref, dtb_ref, dskip_ref, gain_ref, o_ref,
                 st_ref, cx_ref, cbc_ref_, cst_ref):
    g = pl.program_id(1)
    n = x_ref.shape[0]

    @pl.when(pl.program_id(2) == 0)
    def _():
        st_ref[...] = jnp.zeros_like(st_ref)
        cx_ref[...] = jnp.zeros_like(cx_ref)
        cbc_ref_[...] = jnp.zeros_like(cbc_ref_)

    shift_sel = _shift_selector(n)
    xs = _conv_silu(x_ref[...], cwx_ref, cx_ref, shift_sel, cbx_ref[...])
    bcm = _conv_silu(jnp.concatenate([b_ref[...], c_ref[...]], axis=1),
                     jnp.concatenate([cwb_ref[...], cwc_ref[...]], axis=1), cbc_ref_, shift_sel,
                     jnp.concatenate([cbb_ref[...], cbc_ref[...]], axis=1)).astype(BF16)
    bm, cm = bcm[:, :C_DSTATE], bcm[:, C_DSTATE:]

    lane = lax.broadcasted_iota(jnp.int32, (n, LANES), 1)
    dt_all = _softplus(sm_ref[...] + dtb_ref[...])
    da = dt_all * (-jnp.exp(alog_ref[...]))
    _, _, same, causal = _chunk_masks(n)
    cs, _ = _chunk_cumsum(causal, da)
    cst_ref[...] = cs.T
    er = lax.broadcasted_iota(jnp.int32, (LANES, C_GROUP_W), 0)
    ec = lax.broadcasted_iota(jnp.int32, (LANES, C_GROUP_W), 1)
    head0 = S_DT + g * C_HEADS_PER_GROUP
    expand = jnp.where(er == head0 + ec // C_HEADDIM, 1.0, 0.0).astype(BF16)
    dt_e = _dot(dt_all.astype(BF16), expand)
    cs_e = _x_dot_sel(cs, expand)
    csl_e = jnp.concatenate(
        [jnp.broadcast_to(cs_e[c * CHUNK + CHUNK - 1:(c + 1) * CHUNK, :], (CHUNK, C_GROUP_W))
         for c in range(n // CHUNK)], axis=0)
    ecs, edl = jnp.exp(cs_e), jnp.exp(csl_e)
    xdt = xs * dt_e
    xdt_b = xdt.astype(BF16)
    xdtw = (xdt * jnp.exp(csl_e - cs_e)).astype(BF16)
    sub = SUB
    assert n == 2 * sub
    ri = lax.broadcasted_iota(jnp.int32, (sub, n), 0)
    ci = lax.broadcasted_iota(jnp.int32, (sub, n), 1) & (sub - 1)
    causal_c = ((ri >> CHUNK_SHIFT) == (ci >> CHUNK_SHIFT)) & (ci <= ri)
    zblk = jnp.zeros((sub, LANES), BF16)

    def _blockdiag(top, bottom):
        return jnp.concatenate([jnp.concatenate([top, zblk], axis=1),
                                jnp.concatenate([zblk, bottom], axis=1)], axis=0)

    scores = jnp.where(causal_c, _dot_nt(jnp.concatenate([cm[:sub], cm[sub:]], axis=1),
                                         _blockdiag(bm[:sub], bm[sub:])), 0.0)
    half = lax.broadcasted_iota(jnp.int32, (n, LANES), 1) // C_HEADDIM
    y_intra = []
    for pair in range(C_HEADS_PER_GROUP // 2):
        xp = xdt_b[:, pair * LANES:(pair + 1) * LANES]
        acc = None
        for side in range(2):
            hidx = head0 + 2 * pair + side
            ccol = _lane_col(cs, lane, hidx)
            ccol_c = jnp.concatenate([jnp.broadcast_to(ccol[:sub], (sub, sub)),
                                      jnp.broadcast_to(ccol[sub:], (sub, sub))], axis=1)
            crow = cst_ref[pl.ds(hidx, 1), :]
            dec = jnp.exp(jnp.where(causal_c, ccol_c - crow, -jnp.inf))
            xm = jnp.where(half == side, xp, jnp.zeros_like(xp))
            part = _dot((scores * dec).astype(BF16), _blockdiag(xm[:sub], xm[sub:]))
            acc = part if acc is None else acc + part
        y_intra.append(jnp.concatenate([acc[:, :LANES], acc[:, LANES:]], axis=0))
    y = jnp.concatenate(y_intra, axis=1)

    st = st_ref[...]
    rows = [slice(c * CHUNK, (c + 1) * CHUNK) for c in range(n // CHUNK)]
    upd = [_dot_tn(bm[r], xdtw[r]) for r in rows]
    y_inter = []
    for c, r in enumerate(rows):
        y_inter.append(_dot(cm[r], st.astype(BF16)) * ecs[r])
        st = st * edl[c * CHUNK:c * CHUNK + 1, :] + upd[c]
    st_ref[...] = st
    y = y + jnp.concatenate(y_inter, axis=0) + xs * dskip_ref[...]
    yz = y * _silu(z_ref[...].astype(F32))
    yz = yz * lax.rsqrt(jnp.mean(yz * yz, axis=1, keepdims=True) + RMS_EPS) * gain_ref[...]
    o_ref[...] = yz.astype(o_ref.dtype)


def _mixer_c(p, psm, cwx, cbx, cwb, cbb, cwc, cbc, alog, dtb, dskip, gain, bsz, length):
    nl = length // SUPER
    gw, ds = C_GROUP_W, C_DSTATE

    def col(base, width):
        return lambda b, g, l: (b * nl + l, base // width + g)

    grp = lambda b, g, l: (0, g)
    vec = lambda b, g, l: (0, 0)
    return pl.pallas_call(
        _mixc_kernel,
        grid=(bsz, C_GROUPS, nl),
        in_specs=[pl.BlockSpec((SUPER, gw), col(P_ZC, gw)), pl.BlockSpec((SUPER, gw), col(P_XC, gw)),
                  pl.BlockSpec((SUPER, ds), col(P_BC, ds)), pl.BlockSpec((SUPER, ds), col(P_CC, ds)),
                  pl.BlockSpec((SUPER, LANES), lambda b, g, l: (b * nl + l, 0)),
                  pl.BlockSpec((CONV_W, gw), grp), pl.BlockSpec((1, gw), grp),
                  pl.BlockSpec((CONV_W, ds), grp), pl.BlockSpec((1, ds), grp),
                  pl.BlockSpec((CONV_W, ds), grp), pl.BlockSpec((1, ds), grp),
                  pl.BlockSpec((1, LANES), vec), pl.BlockSpec((1, LANES), vec),
                  pl.BlockSpec((1, gw), grp), pl.BlockSpec((1, gw), grp)],
        out_specs=pl.BlockSpec((SUPER, gw), lambda b, g, l: (b * nl + l, g)),
        out_shape=jax.ShapeDtypeStruct((bsz * length, C_DINNER), BF16),
        scratch_shapes=[pltpu.VMEM((ds, gw), F32),
                        pltpu.VMEM((8, gw), F32), pltpu.VMEM((8, 2 * ds), F32),
                        pltpu.VMEM((LANES, SUPER), F32)],
        compiler_params=_cparams(("parallel", "parallel", "arbitrary")),
        name="mixer_c",
    )(p, p, p, p, psm, cwx, cbx, cwb, cbb, cwc, cbc, alog, dtb, dskip, gain)


def _layernorm(y, g, b):
    mu = jnp.mean(y, axis=1, keepdims=True)
    yc = y - mu
    var = jnp.mean(yc * yc, axis=1, keepdims=True)
    return yc * lax.rsqrt(var + LN_EPS) * g + b


def _merge_kernel(oa_ref, ob_ref, oc_ref, ga_ref, gb_ref, gc_ref, woa_ref, wob_ref, woc_ref, m_ref):
    merged = jax.nn.sigmoid(ga_ref[...].astype(F32)) * _dot(oa_ref[...], woa_ref[...])
    merged = merged + jax.nn.sigmoid(gb_ref[...].astype(F32)) * _dot(ob_ref[...], wob_ref[...])
    merged = merged + jax.nn.sigmoid(gc_ref[...].astype(F32)) * _dot(oc_ref[...], woc_ref[...])
    m_ref[...] = merged.astype(m_ref.dtype)


def _merge(oa, ob, oc, p, woa, wob, woc):
    t = oa.shape[0]
    tm = min(POST_TM, t)
    d = D_MODEL
    row = lambda i: (i, 0)
    wspec = lambda shape: pl.BlockSpec(shape, lambda i: (0, 0), pipeline_mode=pl.Buffered(1))
    return pl.pallas_call(
        _merge_kernel,
        grid=(t // tm,),
        in_specs=[pl.BlockSpec((tm, 1024), row), pl.BlockSpec((tm, 1024), row), pl.BlockSpec((tm, 1024), row),
                  pl.BlockSpec((tm, d), lambda i: (i, P_GATES // d)),
                  pl.BlockSpec((tm, d), lambda i: (i, P_GATES // d + 1)),
                  pl.BlockSpec((tm, d), lambda i: (i, P_GATES // d + 2)),
                  wspec((1024, d)), wspec((1024, d)), wspec((1024, d))],
        out_specs=pl.BlockSpec((tm, d), row),
        out_shape=jax.ShapeDtypeStruct((t, d), BF16),
        compiler_params=_cparams(("parallel",)),
        name="merge",
    )(oa, ob, oc, p, p, p, woa, wob, woc)


def _route(logits, carry):
    tm = logits.shape[0]
    lane = lax.broadcasted_iota(jnp.int32, (tm, LANES), 1)
    neg = -jnp.inf
    is_g = lane < N_GROUPS
    gl = jnp.where(is_g, logits, neg)
    gmax = jnp.max(gl, axis=1, keepdims=True)
    gsum = jnp.sum(jnp.where(is_g, jnp.exp(logits - gmax), 0.0), axis=1, keepdims=True)
    p_g = 1.0 / gsum
    g_sel = jnp.min(jnp.where(gl == gmax, lane, LANES), axis=1, keepdims=True)
    in_g = (lane >= N_GROUPS) & (lane < N_GROUPS + N_EXPERTS) & (((lane - N_GROUPS) >> 2) == g_sel)
    emax = jnp.max(jnp.where(in_g, logits, neg), axis=1, keepdims=True)
    ee = jnp.where(in_g, jnp.exp(logits - emax), 0.0)
    probs = jnp.where(in_g, ee / jnp.sum(ee, axis=1, keepdims=True), -1.0)
    p1 = jnp.max(probs, axis=1, keepdims=True)
    i1 = jnp.min(jnp.where(probs == p1, lane, LANES), axis=1, keepdims=True)
    probs2 = jnp.where(lane == i1, -1.0, probs)
    p2 = jnp.max(probs2, axis=1, keepdims=True)
    i2 = jnp.min(jnp.where(probs2 == p2, lane, LANES), axis=1, keepdims=True)
    psum = p1 + p2
    w1 = p_g * (p1 / psum)
    w2 = p_g * (p2 / psum)
    l1 = (i1 - N_GROUPS) & (EXPERTS_PER_GROUP - 1)
    l2 = (i2 - N_GROUPS) & (EXPERTS_PER_GROUP - 1)
    la = jnp.minimum(l1, l2)
    lb = jnp.maximum(l1, l2)
    pair = jnp.where(la == 0, lb - 1, jnp.where(la == 1, 6 - lb, 5))
    seg = g_sel * N_PAIRS + pair
    first_is_a = l1 == jnp.where(pair < 3, PAIR_A[0], jnp.where(pair < 5, PAIR_A[3], PAIR_A[5]))
    wa = jnp.where(first_is_a, w1, w2)
    wb = jnp.where(first_is_a, w2, w1)

    oh = lane == seg
    cnt = jnp.where(oh, 1.0, 0.0)
    ri = lax.broadcasted_iota(jnp.int32, (tm, tm), 0)
    ci = lax.broadcasted_iota(jnp.int32, (tm, tm), 1)
    before = _dot(jnp.where(ci < ri, 1.0, 0.0).astype(BF16), cnt.astype(BF16)) + carry
    rank = jnp.sum(jnp.where(oh, before, 0.0), axis=1, keepdims=True).astype(jnp.int32)
    total = carry + jnp.sum(cnt, axis=0, keepdims=True)
    packed = jnp.where(lane == 0, seg, jnp.where(lane == 1, rank, jnp.zeros_like(lane)))
    return packed.T[0:8, :], jnp.where(lane == 0, wa, jnp.where(lane == 1, wb, 0.0)), total


def _post_kernel(m_ref, x_ref, wout_ref, wr_ref, lng_ref, lnb_ref, x1e_ref, ridx_ref, cnt_ref, carry_ref, y_ref):
    tm = x_ref.shape[0]
    step = pl.program_id(0)

    @pl.when(step == 0)
    def _():
        carry_ref[...] = jnp.zeros_like(carry_ref)
        y_ref[...] = jnp.zeros_like(y_ref)

    cur = step % 2
    y = y_ref[1 - cur]
    y_ref[cur] = ALPHA * x_ref[...] + _dot(m_ref[...], wout_ref[...])
    x1 = _layernorm(y, lng_ref[...], lnb_ref[...])
    x1e_ref[:, :D_MODEL] = x1

    xh = x1.astype(BF16)
    xl = (x1 - xh.astype(F32)).astype(BF16)
    prod = _dot(xh, wr_ref[...]) + _dot(xl, wr_ref[...])
    logits = prod[:, :LANES] + prod[:, LANES:]
    sub = min(ROUTE_ROWS, tm)
    total = carry_ref[0:1, :]
    for r0 in range(0, tm, sub):
        rows_t, w_lanes, total = _route(logits[r0:r0 + sub], total)
        ridx_ref[:, r0:r0 + sub] = rows_t
        x1e_ref[r0:r0 + sub, D_MODEL:] = w_lanes
    total = total * jnp.where(step > 0, 1.0, 0.0)
    carry_ref[...] = jnp.broadcast_to(total, carry_ref.shape)
    cnt_ref[...] = jnp.broadcast_to(total, cnt_ref.shape)


def _post(merged, x, wout, wr, lng, lnb):
    t = x.shape[0]
    tm = min(POST_TM, t)
    d = D_MODEL
    n = t // tm
    row_in = lambda i: (jnp.minimum(i, n - 1), 0)
    row_out = lambda i: (jnp.maximum(i - 1, 0), 0)
    const = lambda i: (0, 0)
    wspec = lambda shape: pl.BlockSpec(shape, const, pipeline_mode=pl.Buffered(1))
    return pl.pallas_call(
        _post_kernel,
        grid=(n + 1,),
        in_specs=[pl.BlockSpec((tm, d), row_in), pl.BlockSpec((tm, d), row_in),
                  wspec((d, d)), wspec((d, 2 * LANES)), wspec((1, d)), wspec((1, d))],
        out_specs=[pl.BlockSpec((tm, XE_W), row_out), pl.BlockSpec((8, tm), lambda i: (0, jnp.maximum(i - 1, 0))),
                   pl.BlockSpec((8, LANES), const)],
        out_shape=[jax.ShapeDtypeStruct((t, XE_W), F32), jax.ShapeDtypeStruct((8, t), jnp.int32),
                   jax.ShapeDtypeStruct((8, LANES), F32)],
        scratch_shapes=[pltpu.VMEM((8, LANES), F32), pltpu.VMEM((2, tm, d), F32)],
        compiler_params=_cparams(("arbitrary",)),
        name="ln_router",
    )(merged, x, wout, wr, lng, lnb)


def _row_copy(src_ref, src_row, dst_ref, dst_row, sem):
    return pltpu.make_async_copy(src_ref.at[pl.ds(src_row, 1), :], dst_ref.at[pl.ds(dst_row, 1), :], sem)


def _dispatch_kernel(slot_ref, pad_lo_ref, pad_hi_ref, x_ref, xs_ref, zbuf_ref, sem):
    tm = x_ref.shape[0] * ROW_GROUP
    step = pl.program_id(0)
    base = step * tm

    @pl.when(step == 0)
    def _():
        zbuf_ref[...] = jnp.zeros_like(zbuf_ref)

        def zero_copy(lo, kk):
            return _row_copy(zbuf_ref, 0, xs_ref, lo + kk, sem)

        def bucket(sg, carry):
            lo = pad_lo_ref[sg]
            n_copies = pad_hi_ref[sg] - lo

            def start(kk, c):
                zero_copy(lo, kk).start()
                return c

            def wait(kk, c):
                zero_copy(lo, kk).wait()
                return c

            lax.fori_loop(0, n_copies, start, 0)
            lax.fori_loop(0, n_copies, wait, 0)
            return carry

        lax.fori_loop(0, N_SEG, bucket, 0)

        zr = zbuf_ref.shape[0]
        tail_lo = pad_hi_ref[N_SEG - 1]

        def tail_copy(kk):
            return pltpu.make_async_copy(zbuf_ref, xs_ref.at[pl.ds(pl.multiple_of(tail_lo + kk * zr, zr), zr), :], sem)

        def tail_start(kk, c):
            tail_copy(kk).start()
            return c

        def tail_wait(kk, c):
            tail_copy(kk).wait()
            return c

        n_tail = (xs_ref.shape[0] - tail_lo) // zr
        lax.fori_loop(0, n_tail, tail_start, 0)
        lax.fori_loop(0, n_tail, tail_wait, 0)

    def start_rows(g, carry):
        r0 = pl.multiple_of(g * ROW_GROUP, ROW_GROUP)
        for u in range(ROW_GROUP):
            pltpu.make_async_copy(x_ref.at[g, pl.ds(u, 1), :],
                                  xs_ref.at[pl.ds(slot_ref[base + r0 + u], 1), :], sem).start(priority=u % 2)
        return carry

    lax.fori_loop(0, tm // ROW_GROUP, start_rows, 0)
    pltpu.make_async_copy(xs_ref.at[pl.ds(0, tm), :], xs_ref.at[pl.ds(0, tm), :], sem).wait()


def _dispatch(slot, pad_lo, pad_hi, x1e, n_slots):
    t = x1e.shape[0]
    tm = min(DISPATCH_TM, t)
    return pl.pallas_call(
        _dispatch_kernel,
        grid_spec=pltpu.PrefetchScalarGridSpec(
            num_scalar_prefetch=3, grid=(t // tm,),
            in_specs=[pl.BlockSpec((tm // ROW_GROUP, ROW_GROUP, XE_W), lambda i, s, lo, hi: (i, 0, 0))],
            out_specs=pl.BlockSpec(memory_space=pl.ANY),
            scratch_shapes=[pltpu.VMEM((ZERO_ROWS, XE_W), F32), pltpu.SemaphoreType.DMA(())]),
        out_shape=jax.ShapeDtypeStruct((n_slots, XE_W), F32),
        compiler_params=_cparams(("arbitrary",)),
        name="moe_dispatch",
    )(slot, pad_lo, pad_hi, x1e.reshape(t // ROW_GROUP, ROW_GROUP, XE_W))


def _expert_kernel(blk_ref, ea_ref, eb_ref, valid_ref, nused_ref, x_ref, wga_ref, wua_ref, wda_ref,
                   wgb_ref, wub_ref, wdb_ref, y_ref):
    del blk_ref, ea_ref, eb_ref
    tm = x_ref.shape[0]
    half = tm // 2
    step = pl.program_id(0)
    used = step < nused_ref[0]
    full = valid_ref[step] > half

    def run(rows):
        x = x_ref[0:rows, :]
        xb = x[:, :D_MODEL].astype(BF16)
        ha = (_silu(_dot(xb, wga_ref[0])) * _dot(xb, wua_ref[0])).astype(BF16)
        hb = (_silu(_dot(xb, wgb_ref[0])) * _dot(xb, wub_ref[0])).astype(BF16)
        wa = x[:, D_MODEL:D_MODEL + 1]
        wb = x[:, D_MODEL + 1:D_MODEL + 2]
        y_ref[0:rows, :] = wa * _dot(ha, wda_ref[0]) + wb * _dot(hb, wdb_ref[0])

    @pl.when(used & full)
    def _():
        run(tm)

    @pl.when(used & jnp.logical_not(full))
    def _():
        run(half)
        y_ref[half:tm, :] = jnp.zeros((tm - half, D_MODEL), F32)

    @pl.when(jnp.logical_not(used))
    def _():
        y_ref[...] = jnp.zeros_like(y_ref)


def _experts(tile_blk, tile_ea, tile_eb, tile_valid, n_used, xs, wg, wu, wd):
    tm = MOE_TM
    n_tiles = tile_blk.shape[0]
    d, de = D_MODEL, D_EXPERT

    def wspec(shape, which):
        return pl.BlockSpec(shape, lambda i, tb, ea, eb, nv, nu: ((ea, eb)[which][i], 0, 0))

    return pl.pallas_call(
        _expert_kernel,
        grid_spec=pltpu.PrefetchScalarGridSpec(
            num_scalar_prefetch=5, grid=(n_tiles,),
            in_specs=[pl.BlockSpec((tm, XE_W), lambda i, tb, ea, eb, nv, nu: (tb[i], 0)),
                      wspec((1, d, de), 0), wspec((1, d, de), 0), wspec((1, de, d), 0),
                      wspec((1, d, de), 1), wspec((1, d, de), 1), wspec((1, de, d), 1)],
            out_specs=pl.BlockSpec((tm, d), lambda i, tb, ea, eb, nv, nu: (i, 0))),
        out_shape=jax.ShapeDtypeStruct((n_tiles * tm, d), F32),
        compiler_params=_cparams(("arbitrary",), EXPERT_VMEM_LIMIT_BYTES),
        name="moe_experts",
    )(tile_blk, tile_ea, tile_eb, tile_valid, n_used, xs, wg, wu, wd, wg, wu, wd)


def _combine_kernel(slot_ref, ys_ref, x1_ref, lng_ref, lnb_ref, x2_ref, x2b_ref, ybuf_ref, sem):
    tm = x1_ref.shape[0]
    step = pl.program_id(0)

    def gather(s):
        buf = s % 2

        def start_rows(g, carry):
            r0 = pl.multiple_of(g * ROW_GROUP, ROW_GROUP)
            for u in range(ROW_GROUP):
                pltpu.make_async_copy(ys_ref.at[pl.ds(slot_ref[s * tm + r0 + u], 1), :],
                                      ybuf_ref.at[buf, g, pl.ds(u, 1), :], sem.at[buf]).start(priority=u % 2)
            return carry

        lax.fori_loop(0, tm // ROW_GROUP, start_rows, 0)

    @pl.when(step == 0)
    def _():
        gather(step)

    @pl.when(step + 1 < pl.num_programs(0))
    def _():
        gather(step + 1)

    buf = step % 2
    pltpu.make_async_copy(ys_ref.at[pl.ds(0, tm), :], ys_ref.at[pl.ds(0, tm), :], sem.at[buf]).wait()
    y = ybuf_ref[buf].reshape(tm, D_MODEL)
    x2 = _layernorm(ALPHA * x1_ref[...] + y, lng_ref[...], lnb_ref[...])
    x2_ref[...] = x2
    x2b_ref[...] = x2.astype(BF16)


def _combine(slot, ys, x1e, lng, lnb):
    t = x1e.shape[0]
    tm = min(COMBINE_TM, t)
    d = D_MODEL
    return pl.pallas_call(
        _combine_kernel,
        grid_spec=pltpu.PrefetchScalarGridSpec(
            num_scalar_prefetch=1, grid=(t // tm,),
            in_specs=[pl.BlockSpec(memory_space=pl.ANY),
                      pl.BlockSpec((tm, d), lambda i, s: (i, 0)),
                      pl.BlockSpec((1, d), lambda i, s: (0, 0)), pl.BlockSpec((1, d), lambda i, s: (0, 0))],
            out_specs=[pl.BlockSpec((tm, d), lambda i, s: (i, 0)), pl.BlockSpec((tm, d), lambda i, s: (i, 0))],
            scratch_shapes=[pltpu.VMEM((2, tm // ROW_GROUP, ROW_GROUP, d), F32), pltpu.SemaphoreType.DMA((2,))]),
        out_shape=[jax.ShapeDtypeStruct((t, d), F32), jax.ShapeDtypeStruct((t, d), BF16)],
        compiler_params=_cparams(("arbitrary",)),
        name="moe_combine_ln",
    )(slot, ys, x1e, lng, lnb)


def _routing_tables(ridx, cnt, t):
    tm = MOE_TM
    n_tiles = t // tm + N_SEG
    counts = cnt[0, :N_SEG].astype(jnp.int32)
    padded = ((counts + tm - 1) // tm) * tm
    ends = jnp.cumsum(padded)
    offs = ends - padded
    slot = offs[ridx[0]] + ridx[1]
    n_used = ends[-1] // tm
    tile_blk = jnp.minimum(jnp.arange(n_tiles, dtype=jnp.int32), n_used - 1)
    tile_seg = jnp.minimum(jnp.sum(tile_blk[:, None] * tm >= ends[None, :], axis=1), N_SEG - 1)
    tile_valid = jnp.clip(offs[tile_seg] + counts[tile_seg] - tile_blk * tm, 0, tm)
    first = (tile_seg // N_PAIRS) * EXPERTS_PER_GROUP
    tile_ea = first + jnp.asarray(PAIR_A, jnp.int32)[tile_seg % N_PAIRS]
    tile_eb = first + jnp.asarray(PAIR_B, jnp.int32)[tile_seg % N_PAIRS]
    i32 = lambda a: a.astype(jnp.int32)
    return (i32(slot), i32(offs + counts), i32(ends), i32(tile_blk), i32(tile_ea), i32(tile_eb), i32(tile_valid),
            i32(n_used.reshape(1)), n_tiles * tm)


def _lanes(vec, start):
    return jnp.zeros((1, LANES), F32).at[0, start:start + vec.shape[0]].set(vec.astype(F32))


def _layer_params(w_in, conv_a_w, a_log_a, dt_bias_a, norm_a, w_lr_b, b_lr_b, norm_b, conv_c_w, conv_c_b,
                  a_log_c, dt_bias_c, d_c, norm_c, w_oa, w_ob, w_oc, w_out, ln1_g, ln1_b, w_rg, w_re,
                  w_gate_e, w_up_e, w_down_e, ln2_g, ln2_b):
    def cols(a, n):
        return w_in[:, a:a + n]

    w_big = jnp.concatenate([
        cols(_GATES, 3 * D_MODEL), cols(_QA, 1024), cols(_KA, 1024), cols(_VA, 1024), cols(_ZA, 1024),
        cols(_QB, 512), cols(_KB, 512), cols(_VB, 1024), cols(_RB, 1024),
        cols(_ZC, 1024), cols(_XBC, 1536)], axis=1).astype(BF16)
    w_small = jnp.concatenate([cols(_BETA, 8), cols(_ALPHA_A, 8), cols(_LRB, 16), cols(_DTC, 16),
                               jnp.zeros((D_MODEL, LANES - 48), F32)], axis=1).astype(BF16)
    w_r = jnp.concatenate([w_rg, w_re, jnp.zeros((D_MODEL, LANES - N_GROUPS - N_EXPERTS), F32)], axis=1)
    w_rh = w_r.astype(BF16)
    w_rl = (w_r - w_rh.astype(F32)).astype(BF16)
    wlr = jnp.zeros((LANES, B_HEADS * B_DK), F32).at[S_LRB:S_LRB + B_GATE_RANK].set(w_lr_b).astype(BF16)
    row = lambda v: v.reshape(1, -1).astype(F32)
    return dict(
        w_big=w_big, w_small=w_small,
        cwq=conv_a_w[:, :1024], cwk=conv_a_w[:, 1024:2048], cwv=conv_a_w[:, 2048:],
        alog_a=_lanes(a_log_a, S_ALPHA), dtb_a=_lanes(dt_bias_a, S_ALPHA), gain_a=row(norm_a),
        wlr=wlr, blr=row(b_lr_b), gain_b=row(norm_b),
        cwx=conv_c_w[:, :C_DINNER], cbx=row(conv_c_b[:C_DINNER]),
        cwb=conv_c_w[:, C_DINNER:C_DINNER + 256], cbb=row(conv_c_b[C_DINNER:C_DINNER + 256]),
        cwc=conv_c_w[:, C_DINNER + 256:], cbc=row(conv_c_b[C_DINNER + 256:]),
        alog_c=_lanes(a_log_c, S_DT), dtb_c=_lanes(dt_bias_c, S_DT),
        dskip=row(jnp.repeat(d_c, C_HEADDIM)), gain_c=row(norm_c),
        woa=w_oa.astype(BF16), wob=w_ob.astype(BF16), woc=w_oc.astype(BF16), wout=w_out.astype(BF16),
        w_r=jnp.concatenate([w_rh, w_rl], axis=1), ln1_g=row(ln1_g), ln1_b=row(ln1_b),
        wg=w_gate_e.astype(BF16), wu=w_up_e.astype(BF16), wd=w_down_e.astype(BF16),
        ln2_g=row(ln2_g), ln2_b=row(ln2_b))


def _layer(x, xb, pr, bsz, length):
    t = bsz * length
    p, psm = _inproj(xb, pr["w_big"], pr["w_small"])
    oa = _mixer_a(p, psm, pr["cwq"], pr["cwk"], pr["cwv"], pr["alog_a"], pr["dtb_a"], pr["gain_a"], bsz, length)
    ob = _mixer_b(p, psm, pr["wlr"], pr["blr"], pr["gain_b"], bsz, length)
    oc = _mixer_c(p, psm, pr["cwx"], pr["cbx"], pr["cwb"], pr["cbb"], pr["cwc"], pr["cbc"],
                  pr["alog_c"], pr["dtb_c"], pr["dskip"], pr["gain_c"], bsz, length)
    merged = _merge(oa, ob, oc, p, pr["woa"], pr["wob"], pr["woc"])
    x1e, ridx, cnt = _post(merged, x, pr["wout"], pr["w_r"], pr["ln1_g"], pr["ln1_b"])
    slot, pad_lo, pad_hi, tile_blk, tile_ea, tile_eb, tile_valid, n_used, n_slots = _routing_tables(ridx, cnt, t)
    xs = _dispatch(slot, pad_lo, pad_hi, x1e, n_slots)
    ys = _experts(tile_blk, tile_ea, tile_eb, tile_valid, n_used, xs, pr["wg"], pr["wu"], pr["wd"])
    return _combine(slot, ys, x1e, pr["ln2_g"], pr["ln2_b"])


def kernel(x, w_in, conv_a_w, a_log_a, dt_bias_a, norm_a, w_lr_b, b_lr_b, norm_b, conv_c_w, conv_c_b, a_log_c,
           dt_bias_c, d_c, norm_c, w_oa, w_ob, w_oc, w_out, ln1_g, ln1_b, w_rg, w_re, w_gate_e, w_up_e,
           w_down_e, ln2_g, ln2_b):
    bsz, length, d = x.shape
    assert d == D_MODEL and length % SUPER == 0
    params = (w_in, conv_a_w, a_log_a, dt_bias_a, norm_a, w_lr_b, b_lr_b, norm_b, conv_c_w, conv_c_b, a_log_c,
              dt_bias_c, d_c, norm_c, w_oa, w_ob, w_oc, w_out, ln1_g, ln1_b, w_rg, w_re, w_gate_e, w_up_e,
              w_down_e, ln2_g, ln2_b)
    xf = x.reshape(bsz * length, d)
    xb = xf.astype(BF16)
    for layer in range(params[0].shape[0]):
        pr = _layer_params(*(p[layer] for p in params))
        xf, xb = _layer(xf, xb, pr, bsz, length)
    return xf.reshape(bsz, length, d)
```
